```python
import jax, jax.numpy as jnp
from jax import lax
import numpy as np

D_MODEL = 4096
BATCH = 2
SEQ = 8192
DEPTH = 1

D_MIX = D_MODEL
A_WIDTH = D_MIX // 2
A_HEADS = 8
A_HEAD_DIM = A_WIDTH // A_HEADS
A_CHUNK = 128
B_WIDTH = D_MIX - A_WIDTH
B_HEADS = 4
B_KEY_WIDTH = B_WIDTH // 2
B_DK = B_KEY_WIDTH // B_HEADS
B_DV = B_WIDTH // B_HEADS
B_GATE_RANK = 16
B_GATE_TAU = 16.0
B_CHUNK = 64
IN_WIDTHS = (A_WIDTH, A_WIDTH, B_KEY_WIDTH, B_KEY_WIDTH, B_WIDTH, B_WIDTH, B_GATE_RANK)
D_IN = A_WIDTH * 2 + B_KEY_WIDTH * 2 + B_WIDTH * 2 + B_GATE_RANK
IN_SPLITS = (A_WIDTH, 2 * A_WIDTH, 2 * A_WIDTH + B_KEY_WIDTH, 2 * A_WIDTH + 2 * B_KEY_WIDTH,
             2 * A_WIDTH + 2 * B_KEY_WIDTH + B_WIDTH, 2 * A_WIDTH + 2 * B_KEY_WIDTH + 2 * B_WIDTH)
N_EXPERTS = 64
TOP_K = 6
N_GROUPS = 8
TOPK_GROUPS = 4
D_EXPERT = 768
ROUTED_SCALE = 2.5
MOE_BLOCK = 128
N_MOD = 6
EPS = 1e-6

kernel_name = "hybrid_sgu_gla_moe_adaln_block"


def rms_norm(x, g):
    xf = x.astype(jnp.float32)
    y = xf * lax.rsqrt(jnp.mean(xf * xf, axis=-1, keepdims=True) + EPS)
    return (y * g.astype(jnp.float32)).astype(x.dtype)


def layer_norm(x, g, b):
    xf = x.astype(jnp.float32)
    mu = jnp.mean(xf, axis=-1, keepdims=True)
    var = jnp.mean(jnp.square(xf - mu), axis=-1, keepdims=True)
    return ((xf - mu) * lax.rsqrt(var + EPS) * g.astype(jnp.float32) + b.astype(jnp.float32)).astype(x.dtype)


def chunked_spatial_gating(u, v, ln_g, ln_b, w_s, b_s, out_g):
    bsz, seq, _ = u.shape
    n = seq // A_CHUNK
    v = layer_norm(v, ln_g, ln_b).reshape(bsz, n, A_CHUNK, A_HEADS, A_HEAD_DIM)
    causal = jnp.tril(jnp.ones((A_CHUNK, A_CHUNK), dtype=bool))
    w = jnp.where(causal[None], w_s, 0.0).astype(v.dtype)
    s = jnp.einsum('hij,bnjhc->bnihc', w, v) + b_s.T[None, None, :, :, None].astype(v.dtype)
    y = u * s.reshape(bsz, seq, A_WIDTH)
    return rms_norm(y, out_g)


def gated_linear_attention(q, k, v, r, g_low, w_g2, b_g2, head_g):
    bsz, seq, _ = q.shape
    n = seq // B_CHUNK

    def heads(t, d):
        return t.reshape(bsz, n, B_CHUNK, B_HEADS, d).transpose(0, 3, 1, 2, 4).astype(jnp.float32)

    gate_logits = (g_low @ w_g2 + b_g2).astype(jnp.float32)
    log_a = jax.nn.log_sigmoid(gate_logits) / B_GATE_TAU
    qh = heads(q, B_DK) * (B_DK ** -0.5)
    kh = heads(k, B_DK)
    vh = heads(v, B_DV)
    cum = jnp.cumsum(heads(log_a, B_DK), axis=3)
    last = cum[:, :, :, -1:, :]
    q_dec = qh * jnp.exp(cum)
    k_dec = kh * jnp.exp(-cum)
    k_state = kh * jnp.exp(last - cum)
    causal = jnp.tril(jnp.ones((B_CHUNK, B_CHUNK), dtype=bool))
    attn = jnp.where(causal, jnp.einsum('bhnid,bhnjd->bhnij', q_dec, k_dec), 0.0)
    o_intra = jnp.einsum('bhnij,bhnjv->bhniv', attn, vh)
    decay = jnp.exp(last[:, :, :, 0, :])

    def step(state, xs):
        q_c, k_c, v_c, d_c = xs
        o = jnp.einsum('bhcd,bhdv->bhcv', q_c, state)
        state = state * d_c[..., None] + jnp.einsum('bhcd,bhcv->bhdv', k_c, v_c)
        return state, o

    state0 = jnp.zeros((bsz, B_HEADS, B_DK, B_DV), jnp.float32)
    mv = lambda t: jnp.moveaxis(t, 2, 0)
    _, o_inter = lax.scan(step, state0, (mv(q_dec), mv(k_state), mv(vh), mv(decay)))
    o = o_intra + jnp.moveaxis(o_inter, 0, 2)
    o = rms_norm(o.transpose(0, 2, 3, 1, 4), head_g).reshape(bsz, seq, B_WIDTH)
    return (o * jax.nn.silu(r.astype(jnp.float32))).astype(q.dtype)


def swiglu(x, w_gate, w_up, w_down):
    return (jax.nn.silu(x @ w_gate) * (x @ w_up)) @ w_down


def route(xf, w_router, router_bias):
    t = xf.shape[0]
    scores = jax.nn.sigmoid(xf.astype(jnp.float32) @ w_router.astype(jnp.float32))
    sel = scores + router_bias.astype(jnp.float32)
    grp = sel.reshape(t, N_GROUPS, N_EXPERTS // N_GROUPS)
    grp_score = jnp.sum(lax.top_k(grp, 2)[0], axis=-1)
    _, grp_idx = lax.top_k(grp_score, TOPK_GROUPS)
    grp_mask = jnp.any(grp_idx[:, :, None] == jnp.arange(N_GROUPS)[None, None, :], axis=1)
    expert_mask = jnp.repeat(grp_mask, N_EXPERTS // N_GROUPS, axis=1)
    _, idx = lax.top_k(jnp.where(expert_mask, sel, -jnp.inf), TOP_K)
    w = jnp.take_along_axis(scores, idx, axis=1)
    w = w / jnp.sum(w, axis=-1, keepdims=True) * ROUTED_SCALE
    return idx, w


def routed_experts(xf, idx, wts, w_gate, w_up, w_down):
    t = xf.shape[0]
    n_assign = t * TOP_K
    flat_e = idx.reshape(-1)
    flat_tok = jnp.repeat(jnp.arange(t, dtype=jnp.int32), TOP_K)
    flat_w = wts.reshape(-1)
    order = jnp.argsort(flat_e)
    se = flat_e[order]
    counts = jnp.bincount(flat_e, length=N_EXPERTS)
    padded = (counts + MOE_BLOCK - 1) // MOE_BLOCK * MOE_BLOCK
    start = jnp.cumsum(counts) - counts
    pend = jnp.cumsum(padded)
    pstart = pend - padded
    dest = pstart[se] + jnp.arange(n_assign) - start[se]
    n_blocks = -(-n_assign // MOE_BLOCK) + N_EXPERTS
    n_slots = n_blocks * MOE_BLOCK
    slot_tok = jnp.zeros((n_slots,), jnp.int32).at[dest].set(flat_tok[order])
    slot_w = jnp.zeros((n_slots,), jnp.float32).at[dest].set(flat_w[order])
    block_e = jnp.minimum(jnp.searchsorted(pend, jnp.arange(n_blocks) * MOE_BLOCK, side='right'),
                          N_EXPERTS - 1)

    def step(acc, blk):
        toks, w, e = blk
        yb = swiglu(xf[toks], w_gate[e], w_up[e], w_down[e])
        return acc.at[toks].add(yb.astype(jnp.float32) * w[:, None]), None

    acc0 = jnp.zeros((t, xf.shape[1]), jnp.float32)
    out, _ = lax.scan(step, acc0, (slot_tok.reshape(n_blocks, MOE_BLOCK),
                                   slot_w.reshape(n_blocks, MOE_BLOCK), block_e))
    return out.astype(xf.dtype)


def setup_inputs(seed: int = 0) -> dict:
    key = jax.random.key(seed)
    ks = jax.random.split(key, 26)
    L, D, f32 = DEPTH, D_MODEL, jnp.float32
    nrm = lambda k, shape, s: jax.random.normal(k, shape, f32) * s
    gain = lambda k, n: 1.0 + nrm(k, (L, n), 0.02)
    return {
        "x": nrm(ks[0], (BATCH, SEQ, D), 1.0),
        "c": nrm(ks[1], (BATCH, D), 1.0),
        "w_ada": nrm(ks[2], (L, D, N_MOD * D), 0.2 * D ** -0.5),
        "b_ada": nrm(ks[3], (L, N_MOD * D), 0.01),
        "g_pre_mix": gain(ks[4], D),
        "g_post_mix": gain(ks[5], D),
        "g_pre_ffn": gain(ks[6], D),
        "g_post_ffn": gain(ks[7], D),
        "w_in": nrm(ks[8], (L, D, D_IN), D ** -0.5),
        "a_ln_g": gain(ks[9], A_WIDTH),
        "a_ln_b": nrm(ks[10], (L, A_WIDTH), 0.01),
        "a_w_s": nrm(ks[11], (L, A_HEADS, A_CHUNK, A_CHUNK), A_CHUNK ** -0.5),
        "a_b_s": 1.0 + nrm(ks[12], (L, A_HEADS, A_CHUNK), 0.02),
        "a_out_g": gain(ks[13], A_WIDTH),
        "b_w_g2": nrm(ks[14], (L, B_GATE_RANK, B_KEY_WIDTH), B_GATE_RANK ** -0.5),
        "b_b_g2": nrm(ks[15], (L, B_KEY_WIDTH), 0.01),
        "b_head_g": gain(ks[16], B_DV),
        "w_out": nrm(ks[17], (L, D_MIX, D), D_MIX ** -0.5),
        "w_router": nrm(ks[18], (L, D, N_EXPERTS), D ** -0.5),
        "router_bias": nrm(ks[19], (L, N_EXPERTS), 0.01),
        "we_gate": nrm(ks[20], (L, N_EXPERTS, D, D_EXPERT), D ** -0.5),
        "we_up": nrm(ks[21], (L, N_EXPERTS, D, D_EXPERT), D ** -0.5),
        "we_down": nrm(ks[22], (L, N_EXPERTS, D_EXPERT, D), D_EXPERT ** -0.5),
        "ws_gate": nrm(ks[23], (L, D, D_EXPERT), D ** -0.5),
        "ws_up": nrm(ks[24], (L, D, D_EXPERT), D ** -0.5),
        "ws_down": nrm(ks[25], (L, D_EXPERT, D), D_EXPERT ** -0.5),
    }


def reference(x, c, w_ada, b_ada, g_pre_mix, g_post_mix, g_pre_ffn, g_post_ffn, w_in,
              a_ln_g, a_ln_b, a_w_s, a_b_s, a_out_g, b_w_g2, b_b_g2, b_head_g, w_out,
              w_router, router_bias, we_gate, we_up, we_down, ws_gate, ws_up, ws_down):
    bsz, seq, d = x.shape
    c_act = jax.nn.silu(c)
    for l in range(DEPTH):
        mod = (c_act @ w_ada[l] + b_ada[l])[:, None, :]
        shift_m, scale_m, gate_m, shift_f, scale_f, gate_f = jnp.split(mod, N_MOD, axis=-1)

        h = rms_norm(x, g_pre_mix[l]) * (1.0 + scale_m) + shift_m
        proj = h @ w_in[l]
        u_a, v_a, q_b, k_b, v_b, r_b, g_b = jnp.split(proj, IN_SPLITS, axis=-1)
        y_a = chunked_spatial_gating(u_a, v_a, a_ln_g[l], a_ln_b[l], a_w_s[l], a_b_s[l], a_out_g[l])
        y_b = gated_linear_attention(q_b, k_b, v_b, r_b, g_b, b_w_g2[l], b_b_g2[l], b_head_g[l])
        y = jnp.concatenate([y_a, y_b.astype(y_a.dtype)], axis=-1) @ w_out[l]
        x = x + gate_m * rms_norm(y, g_post_mix[l])

        h = rms_norm(x, g_pre_ffn[l]) * (1.0 + scale_f) + shift_f
        hf = h.reshape(bsz * seq, d)
        idx, wts = route(hf, w_router[l], router_bias[l])
        y = routed_experts(hf, idx, wts, we_gate[l], we_up[l], we_down[l]) \
            + swiglu(hf, ws_gate[l], ws_up[l], ws_down[l])
        x = x + gate_f * rms_norm(y.reshape(bsz, seq, d), g_post_ffn[l])
    return x
```

```python
import functools

import jax
import jax.numpy as jnp
from jax import lax
from jax.experimental import pallas as pl
from jax.experimental.pallas import tpu as pltpu

F32 = jnp.float32
BF16 = jnp.bfloat16
HIGHEST = lax.Precision.HIGHEST

V7X_LANES = 128
V7X_SUBLANES = 8
V7X_VMEM_LIMIT_BYTES = 58 * 1024 * 1024

A_HEADS = 8
A_CHUNK = 128
B_HEADS = 4
B_GATE_RANK = 16
B_GATE_TAU = 16.0
B_CHUNK = 64
N_EXPERTS = 64
TOP_K = 6
N_GROUPS = 8
TOPK_GROUPS = 4
ROUTED_SCALE = 2.5
N_MOD = 6
EPS = 1e-6

SLAB_PAD = 4


def _slab_rows(d):
    return d // V7X_LANES


def _slab_pitch(d):
    return _slab_rows(d) + SLAB_PAD


SLAB_GROUP = 4
SLAB_K = SLAB_GROUP * V7X_LANES


def _slab_cols(ref, first_tok, n_tok, pitch, kc):
    return jnp.concatenate(
        [ref[pl.ds(first_tok * pitch + kc * SLAB_GROUP + q, n_tok, stride=pitch), :]
         for q in range(SLAB_GROUP)], axis=-1)


def _params(*sem):
    return pltpu.CompilerParams(dimension_semantics=sem,
                                vmem_limit_bytes=V7X_VMEM_LIMIT_BYTES)


def _silu(x):
    return x * jax.nn.sigmoid(x)


ADA_ROWS = 64


def _ada_kernel(c_ref, w_ref, b_ref, o_ref):
    d, nb = c_ref.shape
    tn = w_ref.shape[1]

    def body(i, accs):
        r0 = pl.multiple_of(i * ADA_ROWS, ADA_ROWS)
        w = w_ref[pl.ds(r0, ADA_ROWS), :]
        cc = _silu(c_ref[pl.ds(r0, ADA_ROWS), :])
        out = []
        for b in range(nb):
            p = w * cc[:, b:b + 1]
            out.append(accs[b] + p.reshape(ADA_ROWS // 8, 8, tn).sum(axis=0))
        return tuple(out)

    accs = lax.fori_loop(0, d // ADA_ROWS, body,
                         tuple(jnp.zeros((8, tn), F32) for _ in range(nb)))
    for b in range(nb):
        o_ref[b:b + 1, :] = accs[b].sum(axis=0, keepdims=True) + b_ref[...]


def _ada(c, w_ada, b_ada, tn=1024):
    nb, d = c.shape
    n = w_ada.shape[1]
    return pl.pallas_call(
        _ada_kernel,
        grid=(n // tn,),
        in_specs=[pl.BlockSpec((d, nb), lambda j: (0, 0)),
                  pl.BlockSpec((d, tn), lambda j: (0, j)),
                  pl.BlockSpec((1, tn), lambda j: (0, j))],
        out_specs=pl.BlockSpec((nb, tn), lambda j: (0, j)),
        out_shape=jax.ShapeDtypeStruct((nb, n), F32),
        compiler_params=_params("arbitrary"),
        name="ada",
    )(c.T, w_ada, b_ada.reshape(1, n))


NORM_ROWS = 16


def _rms_rows(x, g):
    ms = jnp.mean(x * x, axis=-1, keepdims=True)
    return x * lax.rsqrt(ms + EPS) * g


def _inproj_kernel(x_ref, g_ref, sc_ref, sh_ref, w_ref, wgl_ref, o_ref, gl_ref, h_ref):
    tm = x_ref.shape[0]

    @pl.when(pl.program_id(1) == 0)
    def _():
        g = g_ref[...]
        sc = 1.0 + sc_ref[0]
        sh = sh_ref[0]

        def body(r, carry):
            rows = pl.ds(pl.multiple_of(r * NORM_ROWS, NORM_ROWS), NORM_ROWS)
            h = _rms_rows(x_ref[rows, :], g) * sc + sh
            h_ref[rows, :] = h.astype(BF16)
            return carry

        lax.fori_loop(0, tm // NORM_ROWS, body, 0)
        gl_ref[...] = jnp.dot(h_ref[...], wgl_ref[...], preferred_element_type=F32)

    o_ref[...] = jnp.dot(h_ref[...], w_ref[...],
                         preferred_element_type=F32).astype(o_ref.dtype)


def _inproj(x2, g, mod3, w_main, w_gl, seq, tm=512, tn=1024):
    t, d = x2.shape
    n = w_main.shape[1]
    per_b = seq // tm
    return pl.pallas_call(
        _inproj_kernel,
        grid=(t // tm, n // tn),
        in_specs=[pl.BlockSpec((tm, d), lambda i, j: (i, 0)),
                  pl.BlockSpec((1, d), lambda i, j: (0, 0)),
                  pl.BlockSpec((1, 1, d), lambda i, j: ((i // per_b) * N_MOD + 1, 0, 0)),
                  pl.BlockSpec((1, 1, d), lambda i, j: ((i // per_b) * N_MOD + 0, 0, 0)),
                  pl.BlockSpec((d, tn), lambda i, j: (0, j)),
                  pl.BlockSpec((d, V7X_LANES), lambda i, j: (0, 0))],
        out_specs=[pl.BlockSpec((tm, tn), lambda i, j: (i, j)),
                   pl.BlockSpec((tm, V7X_LANES), lambda i, j: (i, 0))],
        out_shape=[jax.ShapeDtypeStruct((t, n), BF16),
                   jax.ShapeDtypeStruct((t, V7X_LANES), F32)],
        scratch_shapes=[pltpu.VMEM((tm, d), BF16)],
        compiler_params=_params("arbitrary", "arbitrary"),
        name="inproj",
    )(x2, g, mod3, mod3, w_main, w_gl)


def _mixer_a_kernel(u_ref, v_ref, lng_ref, lnb_ref, ws_ref, bs_ref, og_ref, o_ref, y_ref):
    tc, aw = u_ref.shape
    nh, c, _ = ws_ref.shape
    hd = aw // nh
    row = lax.broadcasted_iota(jnp.int32, (c, c), 0)
    col = lax.broadcasted_iota(jnp.int32, (c, c), 1)
    causal = col <= row
    for ci in range(tc // c):
        rows = slice(ci * c, (ci + 1) * c)
        v = v_ref[rows, :].astype(F32)
        mu = jnp.mean(v, axis=-1, keepdims=True)
        vc = v - mu
        var = jnp.mean(vc * vc, axis=-1, keepdims=True)
        vn = (vc * lax.rsqrt(var + EPS) * lng_ref[...] + lnb_ref[...]).astype(BF16)
        ssq = jnp.zeros((c, 1), F32)
        for h in range(nh):
            cols = slice(h * hd, (h + 1) * hd)
            w = jnp.where(causal, ws_ref[h], 0.0).astype(BF16)
            s = jnp.dot(w, vn[:, cols], preferred_element_type=F32) + bs_ref[:, h:h + 1]
            y = u_ref[rows, cols].astype(F32) * s
            y_ref[:, cols] = y
            ssq = ssq + jnp.sum(y * y, axis=-1, keepdims=True)
        inv = lax.rsqrt(ssq / aw + EPS)
        o_ref[rows, :] = (y_ref[...] * inv * og_ref[...]).astype(o_ref.dtype)


def _mixer_a(proj, ln_g, ln_b, w_s, b_s, out_g, aw, tc=256):
    t = proj.shape[0]
    nh, c, _ = w_s.shape
    return pl.pallas_call(
        _mixer_a_kernel,
        grid=(t // tc,),
        in_specs=[pl.BlockSpec((tc, aw), lambda i: (i, 0)),
                  pl.BlockSpec((tc, aw), lambda i: (i, 1)),
                  pl.BlockSpec((1, aw), lambda i: (0, 0)),
                  pl.BlockSpec((1, aw), lambda i: (0, 0)),
                  pl.BlockSpec((nh, c, c), lambda i: (0, 0, 0)),
                  pl.BlockSpec((c, nh), lambda i: (0, 0)),
                  pl.BlockSpec((1, aw), lambda i: (0, 0))],
        out_specs=pl.BlockSpec((tc, aw), lambda i: (i, 0)),
        out_shape=jax.ShapeDtypeStruct((t, aw), BF16),
        scratch_shapes=[pltpu.VMEM((c, aw), F32)],
        compiler_params=_params("arbitrary"),
        name="mixer_a",
    )(proj, proj, ln_g, ln_b, w_s, b_s.T, out_g)


def _mixer_b_kernel(q_ref, k_ref, v_ref, r_ref, gl_ref, wg2_ref, bg2_ref, hg_ref,
                    o_ref, s_ref):
    tb, kw = q_ref.shape
    vw = v_ref.shape[1]
    nh = s_ref.shape[0]
    dk, dv = kw // nh, vw // nh
    c = B_CHUNK

    @pl.when(pl.program_id(1) == 0)
    def _():
        s_ref[...] = jnp.zeros_like(s_ref)

    row = lax.broadcasted_iota(jnp.int32, (c, c), 0)
    col = lax.broadcasted_iota(jnp.int32, (c, c), 1)
    causal = col <= row
    tril = jnp.where(causal, 1.0, 0.0).astype(F32)
    ones = jnp.ones((c, V7X_LANES), F32)
    for ci in range(tb // c):
        rows = slice(ci * c, (ci + 1) * c)
        logits = jnp.dot(gl_ref[rows, :], wg2_ref[...], precision=HIGHEST,
                         preferred_element_type=F32) + bg2_ref[...]
        log_a = jax.nn.log_sigmoid(logits) / B_GATE_TAU
        cum = jnp.dot(tril, log_a, precision=HIGHEST, preferred_element_type=F32)
        last = cum[c - 1:c, :]
        q = q_ref[rows, :].astype(F32) * (dk ** -0.5)
        k = k_ref[rows, :].astype(F32)
        q_dec = (q * jnp.exp(cum)).astype(BF16)
        k_dec = (k * jnp.exp(-cum)).astype(BF16)
        k_state = (k * jnp.exp(last - cum)).astype(BF16)
        last_col = lax.dot_general(log_a, ones, (((0,), (0,)), ((), ())),
                                   precision=HIGHEST, preferred_element_type=F32)
        decay_col = jnp.exp(last_col)
        for h in range(nh):
            ks = slice(h * dk, (h + 1) * dk)
            vs = slice(h * dv, (h + 1) * dv)
            vh = v_ref[rows, vs]
            attn = lax.dot_general(q_dec[:, ks], k_dec[:, ks], (((1,), (1,)), ((), ())),
                                   preferred_element_type=F32)
            attn = jnp.where(causal, attn, 0.0).astype(BF16)
            state = s_ref[h]
            o = (jnp.dot(attn, vh, preferred_element_type=F32)
                 + jnp.dot(q_dec[:, ks], state.astype(BF16), preferred_element_type=F32))
            kv = lax.dot_general(k_state[:, ks], vh, (((0,), (0,)), ((), ())),
                                 preferred_element_type=F32)
            s_ref[h] = state * decay_col[ks, 0:1] + kv
            o = _rms_rows(o, hg_ref[...])
            o_ref[rows, vs] = (o * _silu(r_ref[rows, vs].astype(F32))).astype(o_ref.dtype)


def _mixer_b(proj, g_low, w_g2p, b_g2, head_g, bsz, seq, kw, vw, tb=256):
    t = proj.shape[0]
    per_b = seq // tb
    q_blk = (2 * vw) // kw
    v_blk = (2 * vw + 2 * kw) // vw
    row = lambda b, n: b * per_b + n
    return pl.pallas_call(
        _mixer_b_kernel,
        grid=(bsz, per_b),
        in_specs=[pl.BlockSpec((tb, kw), lambda b, n: (row(b, n), q_blk)),
                  pl.BlockSpec((tb, kw), lambda b, n: (row(b, n), q_blk + 1)),
                  pl.BlockSpec((tb, vw), lambda b, n: (row(b, n), v_blk)),
                  pl.BlockSpec((tb, vw), lambda b, n: (row(b, n), v_blk + 1)),
                  pl.BlockSpec((tb, V7X_LANES), lambda b, n: (row(b, n), 0)),
                  pl.BlockSpec((V7X_LANES, kw), lambda b, n: (0, 0)),
                  pl.BlockSpec((1, kw), lambda b, n: (0, 0)),
                  pl.BlockSpec((1, vw // B_HEADS), lambda b, n: (0, 0))],
        out_specs=pl.BlockSpec((tb, vw), lambda b, n: (row(b, n), 0)),
        out_shape=jax.ShapeDtypeStruct((t, vw), BF16),
        scratch_shapes=[pltpu.VMEM((B_HEADS, kw // B_HEADS, vw // B_HEADS), F32)],
        compiler_params=_params("arbitrary", "arbitrary"),
        name="mixer_b",
    )(proj, proj, proj, proj, g_low, w_g2p, b_g2, head_g)


def _outproj_kernel(ya_ref, yb_ref, wa_ref, wb_ref, x_ref, gpost_ref, gate_ref,
                    gpre_ref, sc_ref, sh_ref, wr_ref,
                    x1_ref, h2b_ref, h2s_ref, lg_ref, acc_ref):
    j = pl.program_id(1)
    nj, tm, tn = acc_ref.shape
    d = x_ref.shape[1]
    nr, pitch = _slab_rows(d), _slab_pitch(d)
    acc_ref[j] =(jnp.dot(ya_ref[...], wa_ref[...], preferred_element_type=F32)
                  + jnp.dot(yb_ref[...], wb_ref[...], preferred_element_type=F32))

    @pl.when(j == nj - 1)
    def _():
        gpost = gpost_ref[...]
        gate = gate_ref[0]
        gpre = gpre_ref[...]
        sc = 1.0 + sc_ref[0]
        sh = sh_ref[0]

        def body(r, carry):
            r0 = pl.multiple_of(r * NORM_ROWS, NORM_ROWS)
            rows = pl.ds(r0, NORM_ROWS)
            y = jnp.concatenate([acc_ref[jj, rows, :] for jj in range(nj)], axis=-1)
            x1 = x_ref[rows, :] + gate * _rms_rows(y, gpost)
            x1_ref[rows, :] = x1
            h2 = _rms_rows(x1, gpre) * sc + sh
            h2b_ref[rows, :] = h2.astype(BF16)
            for s in range(pitch):
                h2s_ref[pl.ds(r0 * pitch + s, NORM_ROWS, stride=pitch), :] = (
                    h2[:, s * V7X_LANES:(s + 1) * V7X_LANES] if s < nr
                    else jnp.zeros((NORM_ROWS, V7X_LANES), F32))
            return carry

        lax.fori_loop(0, tm // NORM_ROWS, body, 0)
        lg = jnp.zeros(lg_ref.shape, F32)
        for kc in range(nr // SLAB_GROUP):
            lg = lg + jnp.dot(_slab_cols(h2s_ref, 0, tm, pitch, kc),
                              wr_ref[kc * SLAB_K:(kc + 1) * SLAB_K, :],
                              precision=HIGHEST, preferred_element_type=F32)
        lg_ref[...] = lg


def _outproj(ya, yb, w_out, x2, gpost, mod3, gpre, w_rp, seq, tm=256, tn=512):
    t, d = x2.shape
    aw = ya.shape[1]
    per_b = seq // tm
    pitch = _slab_pitch(d)
    modspec = lambda m: pl.BlockSpec((1, 1, d), lambda i, j: ((i // per_b) * N_MOD + m, 0, 0))
    return pl.pallas_call(
        _outproj_kernel,
        grid=(t // tm, d // tn),
        in_specs=[pl.BlockSpec((tm, aw), lambda i, j: (i, 0)),
                  pl.BlockSpec((tm, aw), lambda i, j: (i, 0)),
                  pl.BlockSpec((aw, tn), lambda i, j: (0, j)),
                  pl.BlockSpec((aw, tn), lambda i, j: (1, j)),
                  pl.BlockSpec((tm, d), lambda i, j: (i, 0)),
                  pl.BlockSpec((1, d), lambda i, j: (0, 0)),
                  modspec(2),
                  pl.BlockSpec((1, d), lambda i, j: (0, 0)),
                  modspec(4),
                  modspec(3),
                  pl.BlockSpec((d, V7X_LANES), lambda i, j: (0, 0))],
        out_specs=[pl.BlockSpec((tm, d), lambda i, j: (i, 0)),
                   pl.BlockSpec((tm, d), lambda i, j: (i, 0)),
                   pl.BlockSpec((tm * pitch, V7X_LANES), lambda i, j: (i, 0)),
                   pl.BlockSpec((tm, V7X_LANES), lambda i, j: (i, 0))],
        out_shape=[jax.ShapeDtypeStruct((t, d), F32),
                   jax.ShapeDtypeStruct((t, d), BF16),
                   jax.ShapeDtypeStruct((t * pitch, V7X_LANES), F32),
                   jax.ShapeDtypeStruct((t, V7X_LANES), F32)],
        scratch_shapes=[pltpu.VMEM((d // tn, tm, tn), F32)],
        compiler_params=_params("arbitrary", "arbitrary"),
        name="outproj",
    )(ya, yb, w_out, w_out, x2, gpost, mod3, gpre, mod3, mod3, w_rp)


def _first_argmax(x, iota, axis, size):
    m = jnp.max(x, axis=axis, keepdims=True)
    idx = jnp.min(jnp.where(x == m, iota, size), axis=axis, keepdims=True)
    return m, idx


def _route_kernel(lg_ref, bias_ref, idx_ref, w_ref):
    tt = lg_ref.shape[0]
    ne, ng = N_EXPERTS, N_GROUPS
    gs = ne // ng
    neg = -jnp.inf
    scores = jax.nn.sigmoid(lg_ref[...].T[:ne, :])
    sel = scores + bias_ref[...]
    g3 = sel.reshape(ng, gs, tt)
    j_iota = lax.broadcasted_iota(jnp.int32, (ng, gs, tt), 1)
    m1, i1 = _first_argmax(g3, j_iota, 1, gs)
    m2 = jnp.max(jnp.where(j_iota == i1, neg, g3), axis=1, keepdims=True)
    grp = (m1 + m2).reshape(ng, tt)
    g_iota = lax.broadcasted_iota(jnp.int32, (ng, tt), 0)
    keep = jnp.zeros((ng, tt), jnp.bool_)
    for _ in range(TOPK_GROUPS):
        _, gi = _first_argmax(grp, g_iota, 0, ng)
        hit = g_iota == gi
        keep = jnp.logical_or(keep, hit)
        grp = jnp.where(hit, neg, grp)
    keep3 = jnp.broadcast_to(keep.reshape(ng, 1, tt), (ng, gs, tt))
    cand = jnp.where(keep3, g3, neg).reshape(ne, tt)
    e_iota = lax.broadcasted_iota(jnp.int32, (ne, tt), 0)
    idxs, ws = [], []
    for _ in range(TOP_K):
        _, ei = _first_argmax(cand, e_iota, 0, ne)
        hit = e_iota == ei
        idxs.append(ei)
        ws.append(jnp.sum(jnp.where(hit, scores, 0.0), axis=0, keepdims=True))
        cand = jnp.where(hit, neg, cand)
    total = ws[0]
    for w in ws[1:]:
        total = total + w
    pad = idx_ref.shape[0] - TOP_K
    idx_ref[...] = jnp.concatenate(idxs + [jnp.zeros((pad, tt), jnp.int32)], axis=0)
    w_ref[...] = jnp.concatenate([w / total * ROUTED_SCALE for w in ws]
                                 + [jnp.zeros((pad, tt), F32)], axis=0)


def _route(logits, bias, tt=512):
    t = logits.shape[0]
    return pl.pallas_call(
        _route_kernel,
        grid=(t // tt,),
        in_specs=[pl.BlockSpec((tt, V7X_LANES), lambda i: (i, 0)),
                  pl.BlockSpec((N_EXPERTS, 1), lambda i: (0, 0))],
        out_specs=[pl.BlockSpec((8, tt), lambda i: (0, i)),
                   pl.BlockSpec((8, tt), lambda i: (0, i))],
        out_shape=[jax.ShapeDtypeStruct((8, t), jnp.int32),
                   jax.ShapeDtypeStruct((8, t), F32)],
        compiler_params=_params("arbitrary"),
        name="route",
    )(logits, bias.reshape(N_EXPERTS, 1))


def _row_gather(src_hbm, dst, sem, src_tok, dst_tok, nr, pitch):
    return pltpu.make_async_copy(src_hbm.at[pl.ds(src_tok * pitch, nr)],
                                 dst.at[pl.ds(dst_tok * pitch, nr)], sem)


def _gather_wait(src_hbm, dst, sem, n_tok, nr):
    pltpu.make_async_copy(src_hbm.at[pl.ds(0, n_tok * nr)],
                          dst.at[pl.ds(0, n_tok * nr)], sem).wait()


def _experts_kernel(be_ref, nr_ref, st_ref, h2s_hbm, wg_ref, wu_ref, wd_ref,
                    y_ref, xbuf, sem):
    i = pl.program_id(0)
    n_real = nr_ref[0]
    d = wg_ref.shape[0]
    nr, pitch = _slab_rows(d), _slab_pitch(d)
    bm = xbuf.shape[1] // pitch
    slot = lax.rem(i, 2)

    def start_gather(blk, sl):
        def body(r, carry):
            tok = st_ref[blk * bm + r]
            _row_gather(h2s_hbm, xbuf.at[sl], sem.at[sl], tok, r, nr, pitch).start()
            return carry
        lax.fori_loop(0, bm, body, 0)

    @pl.when(i == 0)
    def _():
        start_gather(0, 0)

    @pl.when(i + 1 < n_real)
    def _():
        start_gather(i + 1, 1 - slot)

    @pl.when(i < n_real)
    def _():
        _gather_wait(h2s_hbm, xbuf.at[slot], sem.at[slot], bm, nr)
        g = jnp.zeros((bm, wg_ref.shape[1]), F32)
        u = jnp.zeros((bm, wu_ref.shape[1]), F32)
        for kc in range(nr // SLAB_GROUP):
            xs = _slab_cols(xbuf.at[slot], 0, bm, pitch, kc).astype(BF16)
            ks = slice(kc * SLAB_K, (kc + 1) * SLAB_K)
            g = g + jnp.dot(xs, wg_ref[ks, :], preferred_element_type=F32)
            u = u + jnp.dot(xs, wu_ref[ks, :], preferred_element_type=F32)
        a = (_silu(g) * u).astype(BF16)
        for kc in range(nr // SLAB_GROUP):
            y = jnp.dot(a, wd_ref[:, kc * SLAB_K:(kc + 1) * SLAB_K],
                        preferred_element_type=F32)
            for q in range(SLAB_GROUP):
                y_ref[pl.ds(kc * SLAB_GROUP + q, bm, stride=pitch), :] = (
                    y[:, q * V7X_LANES:(q + 1) * V7X_LANES])
        for s in range(nr, pitch):
            y_ref[pl.ds(s, bm, stride=pitch), :] = jnp.zeros((bm, V7X_LANES), F32)

    @pl.when(i >= n_real)
    def _():
        y_ref[...] = jnp.zeros_like(y_ref)


def _experts(block_e, n_real, slot_tok, h2s, wg, wu, wd, bm):
    nb = block_e.shape[0]
    ne, d, f = wg.shape
    pitch = _slab_pitch(d)
    live = lambda i, nr: jnp.minimum(i, nr[0] - 1)
    wspec = lambda shape: pl.BlockSpec((None,) + shape,
                                       lambda i, be, nr, st: (be[live(i, nr)], 0, 0))
    return pl.pallas_call(
        _experts_kernel,
        grid_spec=pltpu.PrefetchScalarGridSpec(
            num_scalar_prefetch=3,
            grid=(nb,),
            in_specs=[pl.BlockSpec(memory_space=pl.ANY),
                      wspec((d, f)), wspec((d, f)), wspec((f, d))],
            out_specs=pl.BlockSpec((bm * pitch, V7X_LANES),
                                   lambda i, be, nr, st: (i, 0)),
            scratch_shapes=[pltpu.VMEM((2, bm * pitch, V7X_LANES), F32),
                            pltpu.SemaphoreType.DMA((2,))]),
        out_shape=jax.ShapeDtypeStruct((nb * bm * pitch, V7X_LANES), F32),
        compiler_params=_params("arbitrary"),
        name="experts",
    )(block_e, n_real, slot_tok, h2s, wg, wu, wd)


def _shared_kernel(h_ref, wg_ref, wu_ref, wd_ref, o_ref):
    h = h_ref[...]
    g = jnp.dot(h, wg_ref[...], preferred_element_type=F32)
    u = jnp.dot(h, wu_ref[...], preferred_element_type=F32)
    a = (_silu(g) * u).astype(BF16)
    o_ref[...] = jnp.dot(a, wd_ref[...], preferred_element_type=F32)


def _shared(h2b, wg, wu, wd, tm=256):
    t, d = h2b.shape
    f = wg.shape[1]
    return pl.pallas_call(
        _shared_kernel,
        grid=(t // tm,),
        in_specs=[pl.BlockSpec((tm, d), lambda i: (i, 0)),
                  pl.BlockSpec((d, f), lambda i: (0, 0)),
                  pl.BlockSpec((d, f), lambda i: (0, 0)),
                  pl.BlockSpec((f, d), lambda i: (0, 0))],
        out_specs=pl.BlockSpec((tm, d), lambda i: (i, 0)),
        out_shape=jax.ShapeDtypeStruct((t, d), F32),
        compiler_params=_params("arbitrary"),
        name="shared",
    )(h2b, wg, wu, wd)


def _combine_kernel(dest_ref, ys_hbm, w_ref, sh_ref, x1_ref, g_ref, gate_ref,
                    o_ref, gbuf, ybuf, sem):
    i = pl.program_id(0)
    n = pl.num_programs(0)
    tt, d = x1_ref.shape
    nr, pitch = _slab_rows(d), _slab_pitch(d)
    slot = lax.rem(i, 2)

    def start_gather(blk, sl):
        def body(r, carry):
            for k in range(TOP_K):
                row = dest_ref[(blk * tt + r) * TOP_K + k]
                _row_gather(ys_hbm, gbuf.at[sl], sem.at[sl], row, k * tt + r,
                            nr, pitch).start()
            return carry
        lax.fori_loop(0, tt, body, 0)

    @pl.when(i == 0)
    def _():
        start_gather(0, 0)

    @pl.when(i + 1 < n)
    def _():
        start_gather(i + 1, 1 - slot)

    _gather_wait(ys_hbm, gbuf.at[slot], sem.at[slot], TOP_K * tt, nr)
    for s in range(nr):
        cols = slice(s * V7X_LANES, (s + 1) * V7X_LANES)
        acc = sh_ref[:, cols]
        for k in range(TOP_K):
            rows = gbuf[slot, pl.ds(k * tt * pitch + s, tt, stride=pitch), :]
            acc = acc + rows * w_ref[:, k:k + 1]
        ybuf[:, cols] = acc
    gate = gate_ref[0]
    g = g_ref[...]

    def body(r, carry):
        rows = pl.ds(pl.multiple_of(r * NORM_ROWS, NORM_ROWS), NORM_ROWS)
        o_ref[rows, :] = x1_ref[rows, :] + gate * _rms_rows(ybuf[rows, :], g)
        return carry

    lax.fori_loop(0, tt // NORM_ROWS, body, 0)


def _combine(dest, y_rows, wts, shared, x1, g, mod3, seq, tt=128):
    t, d = x1.shape
    pitch = _slab_pitch(d)
    per_b = seq // tt
    return pl.pallas_call(
        _combine_kernel,
        grid_spec=pltpu.PrefetchScalarGridSpec(
            num_scalar_prefetch=1,
            grid=(t // tt,),
            in_specs=[pl.BlockSpec(memory_space=pl.ANY),
                      pl.BlockSpec((tt, 8), lambda i, ds: (i, 0)),
                      pl.BlockSpec((tt, d), lambda i, ds: (i, 0)),
                      pl.BlockSpec((tt, d), lambda i, ds: (i, 0)),
                      pl.BlockSpec((1, d), lambda i, ds: (0, 0)),
                      pl.BlockSpec((1, 1, d), lambda i, ds: ((i // per_b) * N_MOD + 5, 0, 0))],
            out_specs=pl.BlockSpec((tt, d), lambda i, ds: (i, 0)),
            scratch_shapes=[pltpu.VMEM((2, TOP_K * tt * pitch, V7X_LANES), F32),
                            pltpu.VMEM((tt, d), F32),
                            pltpu.SemaphoreType.DMA((2,))]),
        out_shape=jax.ShapeDtypeStruct((t, d), F32),
        compiler_params=_params("arbitrary"),
        name="combine",
    )(dest, y_rows, wts, shared, x1, g, mod3)


def _dispatch_plan(idx, bm):
    t, k = idx.shape
    n_assign = t * k
    nb = -(-n_assign // bm) + N_EXPERTS
    flat_e = idx.reshape(-1)
    order = jnp.argsort(flat_e, stable=True).astype(jnp.int32)
    se = flat_e[order]
    counts = jnp.bincount(flat_e, length=N_EXPERTS).astype(jnp.int32)
    padded = (counts + bm - 1) // bm * bm
    start = jnp.cumsum(counts) - counts
    pend = jnp.cumsum(padded)
    pstart = pend - padded
    dest_sorted = (pstart[se] + jnp.arange(n_assign, dtype=jnp.int32) - start[se]).astype(jnp.int32)
    dest = jnp.zeros((n_assign,), jnp.int32).at[order].set(dest_sorted)
    slot_tok = jnp.zeros((nb * bm,), jnp.int32).at[dest_sorted].set(order // k)
    n_real = (pend[-1] // bm).astype(jnp.int32).reshape(1)
    block_e = jnp.minimum(
        jnp.searchsorted(pend, jnp.arange(nb, dtype=jnp.int32) * bm, side='right'),
        N_EXPERTS - 1).astype(jnp.int32)
    return block_e, n_real, slot_tok, dest


def kernel(x, c, w_ada, b_ada, g_pre_mix, g_post_mix, g_pre_ffn, g_post_ffn, w_in,
           a_ln_g, a_ln_b, a_w_s, a_b_s, a_out_g, b_w_g2, b_b_g2, b_head_g, w_out,
           w_router, router_bias, we_gate, we_up, we_down, ws_gate, ws_up, ws_down):
    bsz, seq, d = x.shape
    t = bsz * seq
    aw = d // 2
    vw = d - aw
    kw = vw // 2
    n_main = 2 * aw + 2 * kw + 2 * vw
    x2 = x.reshape(t, d)
    for l in range(w_ada.shape[0]):
        mod = _ada(c, w_ada[l], b_ada[l])
        mod3 = mod.reshape(bsz * N_MOD, 1, d)

        w_main = w_in[l][:, :n_main].astype(BF16)
        w_gl = jnp.pad(w_in[l][:, n_main:], ((0, 0), (0, V7X_LANES - B_GATE_RANK))).astype(BF16)
        proj, g_low = _inproj(x2, g_pre_mix[l][None], mod3, w_main, w_gl, seq)
        y_a = _mixer_a(proj, a_ln_g[l][None], a_ln_b[l][None], a_w_s[l], a_b_s[l],
                       a_out_g[l][None], aw)
        w_g2p = jnp.pad(b_w_g2[l], ((0, V7X_LANES - B_GATE_RANK), (0, 0)))
        y_b = _mixer_b(proj, g_low, w_g2p, b_b_g2[l][None], b_head_g[l][None],
                       bsz, seq, kw, vw)

        w_rp = jnp.pad(w_router[l], ((0, 0), (0, V7X_LANES - N_EXPERTS)))
        x1, h2b, h2s, logits = _outproj(y_a, y_b, w_out[l].astype(BF16), x2,
                                        g_post_mix[l][None], mod3, g_pre_ffn[l][None],
                                        w_rp, seq)
        idx8, w8 = _route(logits, router_bias[l])
        bm = 128
        block_e, n_real, slot_tok, dest = _dispatch_plan(idx8[:TOP_K].T, bm)
        y_rows = _experts(block_e, n_real, slot_tok, h2s, we_gate[l].astype(BF16),
                          we_up[l].astype(BF16), we_down[l].astype(BF16), bm)
        shared = _shared(h2b, ws_gate[l].astype(BF16), ws_up[l].astype(BF16),
                         ws_down[l].astype(BF16))
        x2 = _combine(dest, y_rows, w8.T, shared, x1, g_post_ffn[l][None], mod3, seq)
    return x2.reshape(bsz, seq, d)
```

```python
import functools

import jax
import jax.numpy as jnp
from jax import lax
from jax.experimental import pallas as pl
from jax.experimental.pallas import tpu as pltpu

F32 = jnp.float32
BF16 = jnp.bfloat16
HIGHEST = lax.Precision.HIGHEST

V7X_LANES = 128
V7X_SUBLANES = 8
V7X_VMEM_LIMIT_BYTES = 58 * 1024 * 1024

A_HEADS = 8
A_CHUNK = 128
B_HEADS = 4
B_GATE_RANK = 16
B_GATE_TAU = 16.0
B_CHUNK = 64
N_EXPERTS = 64
TOP_K = 6
N_GROUPS = 8
TOPK_GROUPS = 4
ROUTED_SCALE = 2.5
N_MOD = 6
EPS = 1e-6

SLAB_PAD = 4


def _slab_rows(d):
    return d // V7X_LANES


def _slab_pitch(d):
    return _slab_rows(d) + SLAB_PAD


SLAB_GROUP = 4
SLAB_K = SLAB_GROUP * V7X_LANES


def _slab_cols(ref, first_tok, n_tok, pitch, kc):
    return jnp.concatenate(
        [ref[pl.ds(first_tok * pitch + kc * SLAB_GROUP + q, n_tok, stride=pitch), :]
         for q in range(SLAB_GROUP)], axis=-1)


def _params(*sem):
    return pltpu.CompilerParams(dimension_semantics=sem,
                                vmem_limit_bytes=V7X_VMEM_LIMIT_BYTES)


def _silu(x):
    return x * jax.nn.sigmoid(x)


ADA_ROWS = 64


def _ada_kernel(c_ref, w_ref, b_ref, o_ref, cb_ref):
    d, nb = c_ref.shape
    tn = w_ref.shape[1]

    @pl.when(pl.program_id(0) == 0)
    def _():
        def fill(i, carry):
            rows = pl.ds(pl.multiple_of(i * ADA_ROWS, ADA_ROWS), ADA_ROWS)
            cc = _silu(c_ref[rows, :])
            for b in range(nb):
                cb_ref[b, rows, :] = jnp.broadcast_to(cc[:, b:b + 1], (ADA_ROWS, V7X_LANES))
            return carry
        lax.fori_loop(0, d // ADA_ROWS, fill, 0)

    def body(i, accs):
        r0 = pl.multiple_of(i * ADA_ROWS, ADA_ROWS)
        w = w_ref[pl.ds(r0, ADA_ROWS), :]
        out = []
        for b in range(nb):
            cb = cb_ref[b, pl.ds(r0, ADA_ROWS), :]
            cols = []
            for q in range(tn // V7X_LANES):
                p = w[:, q * V7X_LANES:(q + 1) * V7X_LANES] * cb
                cols.append(p.reshape(ADA_ROWS // 8, 8, V7X_LANES).sum(axis=0))
            out.append(accs[b] + jnp.concatenate(cols, axis=-1))
        return tuple(out)

    accs = lax.fori_loop(0, d // ADA_ROWS, body,
                         tuple(jnp.zeros((8, tn), F32) for _ in range(nb)))
    for b in range(nb):
        o_ref[b:b + 1, :] = accs[b].sum(axis=0, keepdims=True) + b_ref[...]


def _ada(c, w_ada, b_ada, tn=1024):
    nb, d = c.shape
    n = w_ada.shape[1]
    return pl.pallas_call(
        _ada_kernel,
        grid=(n // tn,),
        in_specs=[pl.BlockSpec((d, nb), lambda j: (0, 0)),
                  pl.BlockSpec((d, tn), lambda j: (0, j)),
                  pl.BlockSpec((1, tn), lambda j: (0, j))],
        out_specs=pl.BlockSpec((nb, tn), lambda j: (0, j)),
        out_shape=jax.ShapeDtypeStruct((nb, n), F32),
        scratch_shapes=[pltpu.VMEM((nb, d, V7X_LANES), F32)],
        compiler_params=_params("arbitrary"),
        name="ada",
    )(c.T, w_ada, b_ada.reshape(1, n))


NORM_ROWS = 16
NORM_UNROLL = 2


def _rms_rows(x, g):
    ms = jnp.mean(x * x, axis=-1, keepdims=True)
    return x * lax.rsqrt(ms + EPS) * g


NORM_COLS = 512


def _inproj_kernel(x_ref, g_ref, sc_ref, sh_ref, w_ref, wgl_ref, o_ref, gl_ref, h_ref,
                   vec_ref):
    tm, d = x_ref.shape

    @pl.when(pl.program_id(1) == 0)
    def _():
        vec_ref[0:1, :] = g_ref[...] * (1.0 + sc_ref[0])
        vec_ref[1:2, :] = sh_ref[0]

        def body(r, carry):
            groups = [pl.ds(pl.multiple_of((r * NORM_UNROLL + u) * NORM_ROWS, NORM_ROWS),
                            NORM_ROWS) for u in range(NORM_UNROLL)]
            invs = []
            for rows in groups:
                part = jnp.zeros((NORM_ROWS, V7X_LANES), F32)
                for c0 in range(0, d, NORM_COLS):
                    xc = x_ref[rows, c0:c0 + NORM_COLS]
                    part = part + _fold_lanes(xc * xc)
                invs.append(_inv_rms(part, d))
            for rows, inv in zip(groups, invs):
                for c0 in range(0, d, NORM_COLS):
                    cols = slice(c0, c0 + NORM_COLS)
                    h = (x_ref[rows, cols] * inv) * vec_ref[0:1, cols] + vec_ref[1:2, cols]
                    h_ref[rows, cols] = h.astype(BF16)
            return carry

        lax.fori_loop(0, tm // (NORM_ROWS * NORM_UNROLL), body, 0)
        gl_ref[...] = jnp.dot(h_ref[...], wgl_ref[...], preferred_element_type=F32)

    o_ref[...] = jnp.dot(h_ref[...], w_ref[...],
                         preferred_element_type=F32).astype(o_ref.dtype)


def _inproj(x2, g, mod3, w_main, w_gl, seq, tm=512, tn=1024):
    t, d = x2.shape
    n = w_main.shape[1]
    per_b = seq // tm
    return pl.pallas_call(
        _inproj_kernel,
        grid=(t // tm, n // tn),
        in_specs=[pl.BlockSpec((tm, d), lambda i, j: (i, 0)),
                  pl.BlockSpec((1, d), lambda i, j: (0, 0)),
                  pl.BlockSpec((1, 1, d), lambda i, j: ((i // per_b) * N_MOD + 1, 0, 0)),
                  pl.BlockSpec((1, 1, d), lambda i, j: ((i // per_b) * N_MOD + 0, 0, 0)),
                  pl.BlockSpec((d, tn), lambda i, j: (0, j)),
                  pl.BlockSpec((d, V7X_LANES), lambda i, j: (0, 0))],
        out_specs=[pl.BlockSpec((tm, tn), lambda i, j: (i, j)),
                   pl.BlockSpec((tm, V7X_LANES), lambda i, j: (i, 0))],
        out_shape=[jax.ShapeDtypeStruct((t, n), BF16),
                   jax.ShapeDtypeStruct((t, V7X_LANES), F32)],
        scratch_shapes=[pltpu.VMEM((tm, d), BF16), pltpu.VMEM((8, d), F32)],
        compiler_params=_params("arbitrary", "arbitrary"),
        name="inproj",
    )(x2, g, mod3, mod3, w_main, w_gl)


def _mixer_a_kernel(u_ref, v_ref, lng_ref, lnb_ref, ws_ref, bs_ref, og_ref, o_ref, y_ref):
    tc, aw = u_ref.shape
    nh, c, _ = ws_ref.shape
    hd = aw // nh
    row = lax.broadcasted_iota(jnp.int32, (c, c), 0)
    col = lax.broadcasted_iota(jnp.int32, (c, c), 1)
    causal = col <= row
    for ci in range(tc // c):
        rows = slice(ci * c, (ci + 1) * c)
        v = v_ref[rows, :].astype(F32)
        mu = jnp.mean(v, axis=-1, keepdims=True)
        vc = v - mu
        var = jnp.mean(vc * vc, axis=-1, keepdims=True)
        vn = (vc * lax.rsqrt(var + EPS) * lng_ref[...] + lnb_ref[...]).astype(BF16)
        ssq = jnp.zeros((c, 1), F32)
        for h in range(nh):
            cols = slice(h * hd, (h + 1) * hd)
            w = jnp.where(causal, ws_ref[h], 0.0).astype(BF16)
            s = jnp.dot(w, vn[:, cols], preferred_element_type=F32) + bs_ref[:, h:h + 1]
            y = u_ref[rows, cols].astype(F32) * s
            y_ref[:, cols] = y
            ssq = ssq + jnp.sum(y * y, axis=-1, keepdims=True)
        inv = lax.rsqrt(ssq / aw + EPS)
        o_ref[rows, :] = (y_ref[...] * inv * og_ref[...]).astype(o_ref.dtype)


def _mixer_a(proj, ln_g, ln_b, w_s, b_s, out_g, aw, tc=256):
    t = proj.shape[0]
    nh, c, _ = w_s.shape
    return pl.pallas_call(
        _mixer_a_kernel,
        grid=(t // tc,),
        in_specs=[pl.BlockSpec((tc, aw), lambda i: (i, 0)),
                  pl.BlockSpec((tc, aw), lambda i: (i, 1)),
                  pl.BlockSpec((1, aw), lambda i: (0, 0)),
                  pl.BlockSpec((1, aw), lambda i: (0, 0)),
                  pl.BlockSpec((nh, c, c), lambda i: (0, 0, 0)),
                  pl.BlockSpec((c, nh), lambda i: (0, 0)),
                  pl.BlockSpec((1, aw), lambda i: (0, 0))],
        out_specs=pl.BlockSpec((tc, aw), lambda i: (i, 0)),
        out_shape=jax.ShapeDtypeStruct((t, aw), BF16),
        scratch_shapes=[pltpu.VMEM((c, aw), F32)],
        compiler_params=_params("arbitrary"),
        name="mixer_a",
    )(proj, proj, ln_g, ln_b, w_s, b_s.T, out_g)


def _mixer_b_kernel(q_ref, k_ref, v_ref, r_ref, gl_ref, wg2_ref, bg2_ref, hg_ref,
                    o_ref, s_ref):
    tb, kw = q_ref.shape
    vw = v_ref.shape[1]
    nh = s_ref.shape[0]
    dk, dv = kw // nh, vw // nh
    c = B_CHUNK

    @pl.when(pl.program_id(1) == 0)
    def _():
        s_ref[...] = jnp.zeros_like(s_ref)

    row = lax.broadcasted_iota(jnp.int32, (c, c), 0)
    col = lax.broadcasted_iota(jnp.int32, (c, c), 1)
    causal = col <= row
    tril = jnp.where(causal, 1.0, 0.0).astype(F32)
    ones = jnp.ones((c, V7X_LANES), F32)
    for ci in range(tb // c):
        rows = slice(ci * c, (ci + 1) * c)
        logits = jnp.dot(gl_ref[rows, :], wg2_ref[...], precision=HIGHEST,
                         preferred_element_type=F32) + bg2_ref[...]
        log_a = jax.nn.log_sigmoid(logits) / B_GATE_TAU
        cum = jnp.dot(tril, log_a, precision=HIGHEST, preferred_element_type=F32)
        last = cum[c - 1:c, :]
        q = q_ref[rows, :].astype(F32) * (dk ** -0.5)
        k = k_ref[rows, :].astype(F32)
        q_dec = (q * jnp.exp(cum)).astype(BF16)
        k_dec = (k * jnp.exp(-cum)).astype(BF16)
        k_state = (k * jnp.exp(last - cum)).astype(BF16)
        last_col = lax.dot_general(log_a, ones, (((0,), (0,)), ((), ())),
                                   precision=HIGHEST, preferred_element_type=F32)
        decay_col = jnp.exp(last_col)
        for h in range(nh):
            ks = slice(h * dk, (h + 1) * dk)
            vs = slice(h * dv, (h + 1) * dv)
            vh = v_ref[rows, vs]
            attn = lax.dot_general(q_dec[:, ks], k_dec[:, ks], (((1,), (1,)), ((), ())),
                                   preferred_element_type=F32)
            attn = jnp.where(causal, attn, 0.0).astype(BF16)
            state = s_ref[h]
            o = (jnp.dot(attn, vh, preferred_element_type=F32)
                 + jnp.dot(q_dec[:, ks], state.astype(BF16), preferred_element_type=F32))
            kv = lax.dot_general(k_state[:, ks], vh, (((0,), (0,)), ((), ())),
                                 preferred_element_type=F32)
            s_ref[h] = state * decay_col[ks, 0:1] + kv
            o = _rms_rows(o, hg_ref[...])
            o_ref[rows, vs] = (o * _silu(r_ref[rows, vs].astype(F32))).astype(o_ref.dtype)


def _mixer_b(proj, g_low, w_g2p, b_g2, head_g, bsz, seq, kw, vw, tb=256):
    t = proj.shape[0]
    per_b = seq // tb
    q_blk = (2 * vw) // kw
    v_blk = (2 * vw + 2 * kw) // vw
    row = lambda b, n: b * per_b + n
    return pl.pallas_call(
        _mixer_b_kernel,
        grid=(bsz, per_b),
        in_specs=[pl.BlockSpec((tb, kw), lambda b, n: (row(b, n), q_blk)),
                  pl.BlockSpec((tb, kw), lambda b, n: (row(b, n), q_blk + 1)),
                  pl.BlockSpec((tb, vw), lambda b, n: (row(b, n), v_blk)),
                  pl.BlockSpec((tb, vw), lambda b, n: (row(b, n), v_blk + 1)),
                  pl.BlockSpec((tb, V7X_LANES), lambda b, n: (row(b, n), 0)),
                  pl.BlockSpec((V7X_LANES, kw), lambda b, n: (0, 0)),
                  pl.BlockSpec((1, kw), lambda b, n: (0, 0)),
                  pl.BlockSpec((1, vw // B_HEADS), lambda b, n: (0, 0))],
        out_specs=pl.BlockSpec((tb, vw), lambda b, n: (row(b, n), 0)),
        out_shape=jax.ShapeDtypeStruct((t, vw), BF16),
        scratch_shapes=[pltpu.VMEM((B_HEADS, kw // B_HEADS, vw // B_HEADS), F32)],
        compiler_params=_params("arbitrary", "arbitrary"),
        name="mixer_b",
    )(proj, proj, proj, proj, g_low, w_g2p, b_g2, head_g)


def _fold_lanes(v):
    out = v[:, :V7X_LANES]
    for q in range(1, v.shape[1] // V7X_LANES):
        out = out + v[:, q * V7X_LANES:(q + 1) * V7X_LANES]
    return out


def _inv_rms(part, d):
    return lax.rsqrt(jnp.sum(part, axis=-1, keepdims=True) / d + EPS)


def _outproj_kernel(ya_ref, yb_ref, wa_ref, wb_ref, x_ref, gpost_ref, gate_ref,
                    gpre_ref, sc_ref, sh_ref, wr_ref,
                    x1_ref, h2b_ref, h2s_ref, lg_ref, acc_ref, h2lo_ref, vec_ref):
    j = pl.program_id(1)
    nj, tm, tn = acc_ref.shape
    d = x_ref.shape[1]
    nr, pitch = _slab_rows(d), _slab_pitch(d)
    acc_ref[j] = (jnp.dot(ya_ref[...], wa_ref[...], preferred_element_type=F32)
                  + jnp.dot(yb_ref[...], wb_ref[...], preferred_element_type=F32))

    @pl.when(j == nj - 1)
    def _():
        vec_ref[0:1, :] = gpost_ref[...] * gate_ref[0]
        vec_ref[1:2, :] = gpre_ref[...] * (1.0 + sc_ref[0])
        vec_ref[2:3, :] = sh_ref[0]

        def rows_of(r, u):
            return pl.multiple_of((r * NORM_UNROLL + u) * NORM_ROWS, NORM_ROWS)

        def body(r, carry):
            groups = [rows_of(r, u) for u in range(NORM_UNROLL)]
            invs = []
            for r0 in groups:
                rows = pl.ds(r0, NORM_ROWS)
                part = jnp.zeros((NORM_ROWS, V7X_LANES), F32)
                for jj in range(nj):
                    y = acc_ref[jj, rows, :]
                    part = part + _fold_lanes(y * y)
                invs.append(_inv_rms(part, d))
            invs2 = []
            for r0, inv in zip(groups, invs):
                rows = pl.ds(r0, NORM_ROWS)
                part = jnp.zeros((NORM_ROWS, V7X_LANES), F32)
                for jj in range(nj):
                    cols = slice(jj * tn, (jj + 1) * tn)
                    x1 = (x_ref[rows, cols]
                          + (acc_ref[jj, rows, :] * inv) * vec_ref[0:1, cols])
                    x1_ref[rows, cols] = x1
                    part = part + _fold_lanes(x1 * x1)
                invs2.append(_inv_rms(part, d))
            for r0, inv in zip(groups, invs2):
                rows = pl.ds(r0, NORM_ROWS)
                for jj in range(nj):
                    cols = slice(jj * tn, (jj + 1) * tn)
                    h2 = (x1_ref[rows, cols] * inv) * vec_ref[1:2, cols] + vec_ref[2:3, cols]
                    hb = h2.astype(BF16)
                    h2b_ref[rows, cols] = hb
                    h2lo_ref[rows, cols] = (h2 - hb.astype(F32)).astype(BF16)
                    for q in range(tn // V7X_LANES):
                        s = jj * (tn // V7X_LANES) + q
                        h2s_ref[pl.ds(r0 * pitch + s, NORM_ROWS, stride=pitch), :] = (
                            h2[:, q * V7X_LANES:(q + 1) * V7X_LANES])
                for s in range(nr, pitch):
                    h2s_ref[pl.ds(r0 * pitch + s, NORM_ROWS, stride=pitch), :] = (
                        jnp.zeros((NORM_ROWS, V7X_LANES), F32))
            return carry

        lax.fori_loop(0, tm // (NORM_ROWS * NORM_UNROLL), body, 0)
        p = jnp.dot(h2b_ref[...], wr_ref[...], preferred_element_type=F32)
        p_lo = jnp.dot(h2lo_ref[...], wr_ref[:, :V7X_LANES], preferred_element_type=F32)
        lg_ref[...] = p[:, :V7X_LANES] + p[:, V7X_LANES:] + p_lo


def _outproj(ya, yb, w_out, x2, gpost, mod3, gpre, w_rp, seq, tm=256, tn=512):
    t, d = x2.shape
    aw = ya.shape[1]
    per_b = seq // tm
    pitch = _slab_pitch(d)
    modspec = lambda m: pl.BlockSpec((1, 1, d), lambda i, j: ((i // per_b) * N_MOD + m, 0, 0))
    return pl.pallas_call(
        _outproj_kernel,
        grid=(t // tm, d // tn),
        in_specs=[pl.BlockSpec((tm, aw), lambda i, j: (i, 0)),
                  pl.BlockSpec((tm, aw), lambda i, j: (i, 0)),
                  pl.BlockSpec((aw, tn), lambda i, j: (0, j)),
                  pl.BlockSpec((aw, tn), lambda i, j: (1, j)),
                  pl.BlockSpec((tm, d), lambda i, j: (i, 0)),
                  pl.BlockSpec((1, d), lambda i, j: (0, 0)),
                  modspec(2),
                  pl.BlockSpec((1, d), lambda i, j: (0, 0)),
                  modspec(4),
                  modspec(3),
                  pl.BlockSpec((d, 2 * V7X_LANES), lambda i, j: (0, 0))],
        out_specs=[pl.BlockSpec((tm, d), lambda i, j: (i, 0)),
                   pl.BlockSpec((tm, d), lambda i, j: (i, 0)),
                   pl.BlockSpec((tm * pitch, V7X_LANES), lambda i, j: (i, 0)),
                   pl.BlockSpec((tm, V7X_LANES), lambda i, j: (i, 0))],
        out_shape=[jax.ShapeDtypeStruct((t, d), F32),
                   jax.ShapeDtypeStruct((t, d), BF16),
                   jax.ShapeDtypeStruct((t * pitch, V7X_LANES), F32),
                   jax.ShapeDtypeStruct((t, V7X_LANES), F32)],
        scratch_shapes=[pltpu.VMEM((d // tn, tm, tn), F32),
                        pltpu.VMEM((tm, d), BF16),
                        pltpu.VMEM((8, d), F32)],
        compiler_params=_params("arbitrary", "arbitrary"),
        name="outproj",
    )(ya, yb, w_out, w_out, x2, gpost, mod3, gpre, mod3, mod3, w_rp)


def _first_argmax(x, iota, axis, size):
    m = jnp.max(x, axis=axis, keepdims=True)
    idx = jnp.min(jnp.where(x == m, iota, size), axis=axis, keepdims=True)
    return m, idx


def _route_kernel(lg_ref, bias_ref, idx_ref, w_ref, rank_ref, cnt_ref, seen_ref):
    tt = lg_ref.shape[0]
    ne, ng = N_EXPERTS, N_GROUPS

    @pl.when(pl.program_id(0) == 0)
    def _():
        seen_ref[...] = jnp.zeros_like(seen_ref)

    gs = ne // ng
    neg = -jnp.inf
    scores = jax.nn.sigmoid(lg_ref[...].T[:ne, :])
    sel = scores + bias_ref[...]
    g3 = sel.reshape(ng, gs, tt)
    j_iota = lax.broadcasted_iota(jnp.int32, (ng, gs, tt), 1)
    m1, i1 = _first_argmax(g3, j_iota, 1, gs)
    m2 = jnp.max(jnp.where(j_iota == i1, neg, g3), axis=1, keepdims=True)
    grp = (m1 + m2).reshape(ng, tt)
    g_iota = lax.broadcasted_iota(jnp.int32, (ng, tt), 0)
    keep = jnp.zeros((ng, tt), jnp.bool_)
    for _ in range(TOPK_GROUPS):
        _, gi = _first_argmax(grp, g_iota, 0, ng)
        hit = g_iota == gi
        keep = jnp.logical_or(keep, hit)
        grp = jnp.where(hit, neg, grp)
    keep3 = jnp.broadcast_to(keep.reshape(ng, 1, tt), (ng, gs, tt))
    cand = jnp.where(keep3, g3, neg).reshape(ne, tt)
    e_iota = lax.broadcasted_iota(jnp.int32, (ne, tt), 0)
    idxs, ws, hits = [], [], []
    for _ in range(TOP_K):
        _, ei = _first_argmax(cand, e_iota, 0, ne)
        hit = e_iota == ei
        idxs.append(ei)
        hits.append(hit)
        ws.append(jnp.sum(jnp.where(hit, scores, 0.0), axis=0, keepdims=True))
        cand = jnp.where(hit, neg, cand)
    total = ws[0]
    for w in ws[1:]:
        total = total + w
    pad = idx_ref.shape[0] - TOP_K
    idx_ref[...] = jnp.concatenate(idxs + [jnp.zeros((pad, tt), jnp.int32)], axis=0)
    w_ref[...] = jnp.concatenate([w / total * ROUTED_SCALE for w in ws]
                                 + [jnp.zeros((pad, tt), F32)], axis=0)

    chosen = hits[0]
    for hit in hits[1:]:
        chosen = jnp.logical_or(chosen, hit)
    chosen = jnp.where(chosen, 1.0, 0.0)
    src = lax.broadcasted_iota(jnp.int32, (tt, tt), 0)
    dst = lax.broadcasted_iota(jnp.int32, (tt, tt), 1)
    before = jnp.where(src < dst, 1.0, 0.0).astype(BF16)
    rank_e = seen_ref[:, 0:1] + jnp.dot(chosen.astype(BF16), before,
                                        preferred_element_type=F32)
    ranks = [jnp.sum(jnp.where(hit, rank_e, 0.0), axis=0, keepdims=True) for hit in hits]
    rank_ref[...] = jnp.concatenate(ranks + [jnp.zeros((pad, tt), F32)],
                                    axis=0).astype(jnp.int32)
    seen_ref[...] = seen_ref[...] + jnp.sum(chosen, axis=1, keepdims=True)
    cnt_ref[...] = seen_ref[...].astype(jnp.int32)


def _route(logits, bias, tt=512):
    t = logits.shape[0]
    tok_spec = pl.BlockSpec((8, tt), lambda i: (0, i))
    return pl.pallas_call(
        _route_kernel,
        grid=(t // tt,),
        in_specs=[pl.BlockSpec((tt, V7X_LANES), lambda i: (i, 0)),
                  pl.BlockSpec((N_EXPERTS, 1), lambda i: (0, 0))],
        out_specs=[tok_spec, tok_spec, tok_spec,
                   pl.BlockSpec((N_EXPERTS, V7X_LANES), lambda i: (0, 0))],
        out_shape=[jax.ShapeDtypeStruct((8, t), jnp.int32),
                   jax.ShapeDtypeStruct((8, t), F32),
                   jax.ShapeDtypeStruct((8, t), jnp.int32),
                   jax.ShapeDtypeStruct((N_EXPERTS, V7X_LANES), jnp.int32)],
        scratch_shapes=[pltpu.VMEM((N_EXPERTS, V7X_LANES), F32)],
        compiler_params=_params("arbitrary"),
        name="route",
    )(logits, bias.reshape(N_EXPERTS, 1))


def _plan_kernel(idx_ref, rank_ref, cnt_ref, dest_ref, be_ref, nr_ref, *, bm):
    ne = N_EXPERTS
    shift = bm.bit_length() - 1
    counts = cnt_ref[...]
    padded = ((counts + (bm - 1)) >> shift) << shift
    r = lax.broadcasted_iota(jnp.int32, (ne, ne), 0)
    c = lax.broadcasted_iota(jnp.int32, (ne, ne), 1)
    upto = jnp.where(c <= r, 1.0, 0.0)
    pend = jnp.dot(upto, padded.astype(F32), precision=HIGHEST,
                   preferred_element_type=F32).astype(jnp.int32)
    pstart = pend - padded
    idx = idx_ref[...]
    dest = rank_ref[...]
    blk_row0 = lax.broadcasted_iota(jnp.int32, be_ref.shape, 1) * bm
    blk_e = jnp.zeros(be_ref.shape, jnp.int32)
    for e in range(ne):
        dest = dest + jnp.where(idx == e, pstart[e:e + 1, 0:1], 0)
        blk_e = blk_e + jnp.where(pend[e:e + 1, 0:1] <= blk_row0, 1, 0)
    dest_ref[...] = dest
    be_ref[...] = jnp.minimum(blk_e, ne - 1)
    nr_ref[...] = jnp.broadcast_to(pend[ne - 1:ne, :] >> shift, nr_ref.shape)


def _plan(idx8, rank8, cnt, bm, nb, tt=2048):
    t = idx8.shape[1]
    tt = min(tt, t)
    nbp = -(-nb // V7X_LANES) * V7X_LANES
    tok_spec = pl.BlockSpec((8, tt), lambda i: (0, i))
    return pl.pallas_call(
        functools.partial(_plan_kernel, bm=bm),
        grid=(t // tt,),
        in_specs=[tok_spec, tok_spec,
                  pl.BlockSpec((N_EXPERTS, V7X_LANES), lambda i: (0, 0))],
        out_specs=[tok_spec,
                   pl.BlockSpec((8, nbp), lambda i: (0, 0)),
                   pl.BlockSpec((8, V7X_LANES), lambda i: (0, 0))],
        out_shape=[jax.ShapeDtypeStruct((8, t), jnp.int32),
                   jax.ShapeDtypeStruct((8, nbp), jnp.int32),
                   jax.ShapeDtypeStruct((8, V7X_LANES), jnp.int32)],
        compiler_params=_params("arbitrary"),
        name="plan",
    )(idx8, rank8, cnt)


def _slots_kernel(cnt_ref, dest_ref, st_ref, *, bm):
    i = pl.program_id(0)
    tt = dest_ref.shape[1]
    ns = st_ref.shape[0]

    @pl.when(i == 0)
    def _():
        def clear(lo, hi):
            def body(s, carry):
                st_ref[s] = 0
                return carry
            lax.fori_loop(lo, hi, body, 0)

        def per_expert(e, seg_start):
            n = cnt_ref[e]
            seg_end = seg_start + (n + (bm - 1)) // bm * bm
            clear(seg_start + n, seg_end)
            return seg_end

        used = lax.fori_loop(0, N_EXPERTS, per_expert, 0)
        clear(used, ns)

    def body(t, carry):
        for k in range(TOP_K):
            st_ref[dest_ref[k, t]] = i * tt + t
        return carry

    lax.fori_loop(0, tt, body, 0, unroll=8)


def _slots(counts, dest8, bm, nb, tt=2048):
    t = dest8.shape[1]
    tt = min(tt, t)
    return pl.pallas_call(
        functools.partial(_slots_kernel, bm=bm),
        grid=(t // tt,),
        in_specs=[pl.BlockSpec(memory_space=pltpu.SMEM),
                  pl.BlockSpec((8, tt), lambda i: (0, i), memory_space=pltpu.SMEM)],
        out_specs=pl.BlockSpec(memory_space=pltpu.SMEM),
        out_shape=jax.ShapeDtypeStruct((nb * bm,), jnp.int32),
        compiler_params=_params("arbitrary"),
        name="slots",
    )(counts, dest8)


def _row_gather(src_hbm, dst, sem, src_tok, dst_tok, nr, pitch):
    return pltpu.make_async_copy(src_hbm.at[pl.ds(src_tok * pitch, nr)],
                                 dst.at[pl.ds(dst_tok * pitch, nr)], sem)


def _gather_wait(src_hbm, dst, sem, n_tok, nr):
    pltpu.make_async_copy(src_hbm.at[pl.ds(0, n_tok * nr)],
                          dst.at[pl.ds(0, n_tok * nr)], sem).wait()


def _experts_kernel(be_ref, nr_ref, st_ref, h2s_hbm, wg_ref, wu_ref, wd_ref,
                    y_ref, xbuf, sem):
    i = pl.program_id(0)
    n_real = nr_ref[0]
    d = wg_ref.shape[0]
    nr, pitch = _slab_rows(d), _slab_pitch(d)
    bm = xbuf.shape[1] // pitch
    slot = lax.rem(i, 2)

    def start_gather(blk, sl):
        def body(r, carry):
            tok = st_ref[blk * bm + r]
            _row_gather(h2s_hbm, xbuf.at[sl], sem.at[sl], tok, r, nr, pitch).start()
            return carry
        lax.fori_loop(0, bm, body, 0, unroll=8)

    @pl.when(i == 0)
    def _():
        start_gather(0, 0)

    @pl.when(i + 1 < n_real)
    def _():
        start_gather(i + 1, 1 - slot)

    @pl.when(i < n_real)
    def _():
        _gather_wait(h2s_hbm, xbuf.at[slot], sem.at[slot], bm, nr)
        g = jnp.zeros((bm, wg_ref.shape[1]), F32)
        u = jnp.zeros((bm, wu_ref.shape[1]), F32)
        for kc in range(nr // SLAB_GROUP):
            xs = _slab_cols(xbuf.at[slot], 0, bm, pitch, kc).astype(BF16)
            ks = slice(kc * SLAB_K, (kc + 1) * SLAB_K)
            g = g + jnp.dot(xs, wg_ref[ks, :], preferred_element_type=F32)
            u = u + jnp.dot(xs, wu_ref[ks, :], preferred_element_type=F32)
        a = (_silu(g) * u).astype(BF16)
        for kc in range(nr // SLAB_GROUP):
            y = jnp.dot(a, wd_ref[:, kc * SLAB_K:(kc + 1) * SLAB_K],
                        preferred_element_type=F32)
            for q in range(SLAB_GROUP):
                y_ref[pl.ds(kc * SLAB_GROUP + q, bm, stride=pitch), :] = (
                    y[:, q * V7X_LANES:(q + 1) * V7X_LANES])
        for s in range(nr, pitch):
            y_ref[pl.ds(s, bm, stride=pitch), :] = jnp.zeros((bm, V7X_LANES), F32)

    @pl.when(i >= n_real)
    def _():
        y_ref[...] = jnp.zeros_like(y_ref)


def _experts(block_e, n_real, slot_tok, h2s, wg, wu, wd, bm):
    nb = block_e.shape[0]
    ne, d, f = wg.shape
    pitch = _slab_pitch(d)
    live = lambda i, nr: jnp.minimum(i, nr[0] - 1)
    wspec = lambda shape: pl.BlockSpec((None,) + shape,
                                       lambda i, be, nr, st: (be[live(i, nr)], 0, 0))
    return pl.pallas_call(
        _experts_kernel,
        grid_spec=pltpu.PrefetchScalarGridSpec(
            num_scalar_prefetch=3,
            grid=(nb,),
            in_specs=[pl.BlockSpec(memory_space=pl.ANY),
                      wspec((d, f)), wspec((d, f)), wspec((f, d))],
            out_specs=pl.BlockSpec((bm * pitch, V7X_LANES),
                                   lambda i, be, nr, st: (i, 0)),
            scratch_shapes=[pltpu.VMEM((2, bm * pitch, V7X_LANES), F32),
                            pltpu.SemaphoreType.DMA((2,))]),
        out_shape=jax.ShapeDtypeStruct((nb * bm * pitch, V7X_LANES), F32),
        compiler_params=_params("arbitrary"),
        name="experts",
    )(block_e, n_real, slot_tok, h2s, wg, wu, wd)


def _shared_kernel(h_ref, wg_ref, wu_ref, wd_ref, o_ref):
    h = h_ref[...]
    g = jnp.dot(h, wg_ref[...], preferred_element_type=F32)
    u = jnp.dot(h, wu_ref[...], preferred_element_type=F32)
    a = (_silu(g) * u).astype(BF16)
    o_ref[...] = jnp.dot(a, wd_ref[...], preferred_element_type=F32)


def _shared(h2b, wg, wu, wd, tm=256):
    t, d = h2b.shape
    f = wg.shape[1]
    return pl.pallas_call(
        _shared_kernel,
        grid=(t // tm,),
        in_specs=[pl.BlockSpec((tm, d), lambda i: (i, 0)),
                  pl.BlockSpec((d, f), lambda i: (0, 0)),
                  pl.BlockSpec((d, f), lambda i: (0, 0)),
                  pl.BlockSpec((f, d), lambda i: (0, 0))],
        out_specs=pl.BlockSpec((tm, d), lambda i: (i, 0)),
        out_shape=jax.ShapeDtypeStruct((t, d), F32),
        compiler_params=_params("arbitrary"),
        name="shared",
    )(h2b, wg, wu, wd)


def _combine_kernel(dest_ref, ys_hbm, w_ref, sh_ref, x1_ref, g_ref, gate_ref,
                    o_ref, gbuf, ybuf, vec_ref, sem):
    i = pl.program_id(0)
    n = pl.num_programs(0)
    tt, d = x1_ref.shape
    nr, pitch = _slab_rows(d), _slab_pitch(d)
    slot = lax.rem(i, 2)

    def start_gather(blk, sl):
        def body(r, carry):
            for k in range(TOP_K):
                row = dest_ref[k * (n * tt) + blk * tt + r]
                _row_gather(ys_hbm, gbuf.at[sl], sem.at[sl], row, k * tt + r,
                            nr, pitch).start()
            return carry
        lax.fori_loop(0, tt, body, 0, unroll=4)

    @pl.when(i == 0)
    def _():
        start_gather(0, 0)

    @pl.when(i + 1 < n)
    def _():
        start_gather(i + 1, 1 - slot)

    _gather_wait(ys_hbm, gbuf.at[slot], sem.at[slot], TOP_K * tt, nr)
    wk = [jnp.broadcast_to(w_ref[:, k:k + 1], (tt, V7X_LANES)) for k in range(TOP_K)]
    for s in range(nr):
        cols = slice(s * V7X_LANES, (s + 1) * V7X_LANES)
        acc = sh_ref[:, cols]
        for k in range(TOP_K):
            rows = gbuf[slot, pl.ds(k * tt * pitch + s, tt, stride=pitch), :]
            acc = acc + rows * wk[k]
        ybuf[:, cols] = acc
    vec_ref[0:1, :] = g_ref[...] * gate_ref[0]

    def body(r, carry):
        groups = [pl.ds(pl.multiple_of((r * NORM_UNROLL + u) * NORM_ROWS, NORM_ROWS),
                        NORM_ROWS) for u in range(NORM_UNROLL)]
        invs = []
        for rows in groups:
            part = jnp.zeros((NORM_ROWS, V7X_LANES), F32)
            for c0 in range(0, d, NORM_COLS):
                yc = ybuf[rows, c0:c0 + NORM_COLS]
                part = part + _fold_lanes(yc * yc)
            invs.append(_inv_rms(part, d))
        for rows, inv in zip(groups, invs):
            for c0 in range(0, d, NORM_COLS):
                cols = slice(c0, c0 + NORM_COLS)
                o_ref[rows, cols] = (x1_ref[rows, cols]
                                     + (ybuf[rows, cols] * inv) * vec_ref[0:1, cols])
        return carry

    lax.fori_loop(0, tt // (NORM_ROWS * NORM_UNROLL), body, 0)


def _combine(dest, y_rows, wts, shared, x1, g, mod3, seq, tt=128):
    t, d = x1.shape
    pitch = _slab_pitch(d)
    per_b = seq // tt
    return pl.pallas_call(
        _combine_kernel,
        grid_spec=pltpu.PrefetchScalarGridSpec(
            num_scalar_prefetch=1,
            grid=(t // tt,),
            in_specs=[pl.BlockSpec(memory_space=pl.ANY),
                      pl.BlockSpec((tt, 8), lambda i, ds: (i, 0)),
                      pl.BlockSpec((tt, d), lambda i, ds: (i, 0)),
                      pl.BlockSpec((tt, d), lambda i, ds: (i, 0)),
                      pl.BlockSpec((1, d), lambda i, ds: (0, 0)),
                      pl.BlockSpec((1, 1, d), lambda i, ds: ((i // per_b) * N_MOD + 5, 0, 0))],
            out_specs=pl.BlockSpec((tt, d), lambda i, ds: (i, 0)),
            scratch_shapes=[pltpu.VMEM((2, TOP_K * tt * pitch, V7X_LANES), F32),
                            pltpu.VMEM((tt, d), F32),
                            pltpu.VMEM((8, d), F32),
                            pltpu.SemaphoreType.DMA((2,))]),
        out_shape=jax.ShapeDtypeStruct((t, d), F32),
        compiler_params=_params("arbitrary"),
        name="combine",
    )(dest, y_rows, wts, shared, x1, g, mod3)


EXPERT_BLOCK_ROWS = 128


def kernel(x, c, w_ada, b_ada, g_pre_mix, g_post_mix, g_pre_ffn, g_post_ffn, w_in,
           a_ln_g, a_ln_b, a_w_s, a_b_s, a_out_g, b_w_g2, b_b_g2, b_head_g, w_out,
           w_router, router_bias, we_gate, we_up, we_down, ws_gate, ws_up, ws_down):
    bsz, seq, d = x.shape
    t = bsz * seq
    aw = d // 2
    vw = d - aw
    kw = vw // 2
    n_main = 2 * aw + 2 * kw + 2 * vw
    x2 = x.reshape(t, d)
    for l in range(w_ada.shape[0]):
        mod = _ada(c, w_ada[l], b_ada[l])
        mod3 = mod.reshape(bsz * N_MOD, 1, d)

        w_main = w_in[l][:, :n_main].astype(BF16)
        w_gl = jnp.pad(w_in[l][:, n_main:], ((0, 0), (0, V7X_LANES - B_GATE_RANK))).astype(BF16)
        proj, g_low = _inproj(x2, g_pre_mix[l][None], mod3, w_main, w_gl, seq)
        y_a = _mixer_a(proj, a_ln_g[l][None], a_ln_b[l][None], a_w_s[l], a_b_s[l],
                       a_out_g[l][None], aw)
        w_g2p = jnp.pad(b_w_g2[l], ((0, V7X_LANES - B_GATE_RANK), (0, 0)))
        y_b = _mixer_b(proj, g_low, w_g2p, b_b_g2[l][None], b_head_g[l][None],
                       bsz, seq, kw, vw)

        w_rp = jnp.pad(w_router[l], ((0, 0), (0, V7X_LANES - N_EXPERTS)))
        w_r_hi = w_rp.astype(BF16)
        w_r_lo = (w_rp - w_r_hi.astype(F32)).astype(BF16)
        w_rp = jnp.concatenate([w_r_hi, w_r_lo], axis=1)
        x1, h2b, h2s, logits = _outproj(y_a, y_b, w_out[l].astype(BF16), x2,
                                        g_post_mix[l][None], mod3, g_pre_ffn[l][None],
                                        w_rp, seq)
        idx8, w8, rank8, cnt = _route(logits, router_bias[l])
        bm = EXPERT_BLOCK_ROWS
        nb = -(-t * TOP_K // bm) + N_EXPERTS
        dest8, be8, nr8 = _plan(idx8, rank8, cnt, bm, nb)
        slot_tok = _slots(cnt[:, 0], dest8, bm, nb)
        y_rows = _experts(be8[0, :nb], nr8[0, :1], slot_tok, h2s, we_gate[l].astype(BF16),
                          we_up[l].astype(BF16), we_down[l].astype(BF16), bm)
        shared = _shared(h2b, ws_gate[l].astype(BF16), ws_up[l].astype(BF16),
                         ws_down[l].astype(BF16))
        x2 = _combine(dest8[:TOP_K].reshape(-1), y_rows, w8.T, shared, x1, g_post_ffn[l][None],
                      mod3, seq)
    return x2.reshape(bsz, seq, d)
```

```python
import functools

import jax
import jax.numpy as jnp
from jax import lax
from jax.experimental import pallas as pl
from jax.experimental.pallas import tpu as pltpu

F32 = jnp.float32
BF16 = jnp.bfloat16
HIGHEST = lax.Precision.HIGHEST

V7X_LANES = 128
V7X_SUBLANES = 8
V7X_VMEM_LIMIT_BYTES = 58 * 1024 * 1024

A_HEADS = 8
A_CHUNK = 128
B_HEADS = 4
B_GATE_RANK = 16
B_GATE_TAU = 16.0
B_CHUNK = 64
N_EXPERTS = 64
TOP_K = 6
N_GROUPS = 8
TOPK_GROUPS = 4
ROUTED_SCALE = 2.5
N_MOD = 6
EPS = 1e-6

SLAB_PAD = 4


def _slab_rows(d):
    return d // V7X_LANES


def _slab_pitch(d):
    return _slab_rows(d) + SLAB_PAD


SLAB_GROUP = 4
SLAB_K = SLAB_GROUP * V7X_LANES


def _slab_cols(ref, first_tok, n_tok, pitch, kc):
    return jnp.concatenate(
        [ref[pl.ds(first_tok * pitch + kc * SLAB_GROUP + q, n_tok, stride=pitch), :]
         for q in range(SLAB_GROUP)], axis=-1)


def _params(*sem):
    return pltpu.CompilerParams(dimension_semantics=sem,
                                vmem_limit_bytes=V7X_VMEM_LIMIT_BYTES)


def _silu(x):
    return x * jax.nn.sigmoid(x)


ADA_ROWS = 64


def _ada_kernel(c_ref, w_ref, b_ref, o_ref, cb_ref):
    d, nb = c_ref.shape
    tn = w_ref.shape[1]

    @pl.when(pl.program_id(0) == 0)
    def _():
        def fill(i, carry):
            rows = pl.ds(pl.multiple_of(i * ADA_ROWS, ADA_ROWS), ADA_ROWS)
            cc = _silu(c_ref[rows, :])
            for b in range(nb):
                cb_ref[b, rows, :] = jnp.broadcast_to(cc[:, b:b + 1], (ADA_ROWS, V7X_LANES))
            return carry
        lax.fori_loop(0, d // ADA_ROWS, fill, 0)

    def body(i, accs):
        r0 = pl.multiple_of(i * ADA_ROWS, ADA_ROWS)
        w = w_ref[pl.ds(r0, ADA_ROWS), :]
        out = []
        for b in range(nb):
            cb = cb_ref[b, pl.ds(r0, ADA_ROWS), :]
            cols = []
            for q in range(tn // V7X_LANES):
                p = w[:, q * V7X_LANES:(q + 1) * V7X_LANES] * cb
                cols.append(p.reshape(ADA_ROWS // 8, 8, V7X_LANES).sum(axis=0))
            out.append(accs[b] + jnp.concatenate(cols, axis=-1))
        return tuple(out)

    accs = lax.fori_loop(0, d // ADA_ROWS, body,
                         tuple(jnp.zeros((8, tn), F32) for _ in range(nb)))
    for b in range(nb):
        o_ref[b:b + 1, :] = accs[b].sum(axis=0, keepdims=True) + b_ref[...]


def _ada(c, w_ada, b_ada, tn=1024):
    nb, d = c.shape
    n = w_ada.shape[1]
    return pl.pallas_call(
        _ada_kernel,
        grid=(n // tn,),
        in_specs=[pl.BlockSpec((d, nb), lambda j: (0, 0)),
                  pl.BlockSpec((d, tn), lambda j: (0, j)),
                  pl.BlockSpec((1, tn), lambda j: (0, j))],
        out_specs=pl.BlockSpec((nb, tn), lambda j: (0, j)),
        out_shape=jax.ShapeDtypeStruct((nb, n), F32),
        scratch_shapes=[pltpu.VMEM((nb, d, V7X_LANES), F32)],
        compiler_params=_params("arbitrary"),
        name="ada",
    )(c.T, w_ada, b_ada.reshape(1, n))


NORM_ROWS = 16
NORM_UNROLL = 2


def _rms_rows(x, g):
    ms = jnp.mean(x * x, axis=-1, keepdims=True)
    return x * lax.rsqrt(ms + EPS) * g


NORM_COLS = 512


def _inproj_kernel(x_ref, g_ref, sc_ref, sh_ref, w_ref, wgl_ref, o_ref, gl_ref, h_ref,
                   vec_ref):
    tm, d = x_ref.shape

    @pl.when(pl.program_id(1) == 0)
    def _():
        vec_ref[0:1, :] = g_ref[...] * (1.0 + sc_ref[0])
        vec_ref[1:2, :] = sh_ref[0]

        def body(r, carry):
            groups = [pl.ds(pl.multiple_of((r * NORM_UNROLL + u) * NORM_ROWS, NORM_ROWS),
                            NORM_ROWS) for u in range(NORM_UNROLL)]
            invs = []
            for rows in groups:
                part = jnp.zeros((NORM_ROWS, V7X_LANES), F32)
                for c0 in range(0, d, NORM_COLS):
                    xc = x_ref[rows, c0:c0 + NORM_COLS]
                    part = part + _fold_lanes(xc * xc)
                invs.append(_inv_rms(part, d))
            for rows, inv in zip(groups, invs):
                for c0 in range(0, d, NORM_COLS):
                    cols = slice(c0, c0 + NORM_COLS)
                    h = (x_ref[rows, cols] * inv) * vec_ref[0:1, cols] + vec_ref[1:2, cols]
                    h_ref[rows, cols] = h.astype(BF16)
            return carry

        lax.fori_loop(0, tm // (NORM_ROWS * NORM_UNROLL), body, 0)
        gl_ref[...] = jnp.dot(h_ref[...], wgl_ref[...], preferred_element_type=F32)

    o_ref[...] = jnp.dot(h_ref[...], w_ref[...],
                         preferred_element_type=F32).astype(o_ref.dtype)


def _inproj(x2, g, mod3, w_all, n, w_gl, seq, tm=512, tn=1024):
    t, d = x2.shape
    per_b = seq // tm
    return pl.pallas_call(
        _inproj_kernel,
        grid=(t // tm, n // tn),
        in_specs=[pl.BlockSpec((tm, d), lambda i, j: (i, 0)),
                  pl.BlockSpec((1, d), lambda i, j: (0, 0)),
                  pl.BlockSpec((1, 1, d), lambda i, j: ((i // per_b) * N_MOD + 1, 0, 0)),
                  pl.BlockSpec((1, 1, d), lambda i, j: ((i // per_b) * N_MOD + 0, 0, 0)),
                  pl.BlockSpec((d, tn), lambda i, j: (0, j)),
                  pl.BlockSpec((d, V7X_LANES), lambda i, j: (0, 0))],
        out_specs=[pl.BlockSpec((tm, tn), lambda i, j: (i, j)),
                   pl.BlockSpec((tm, V7X_LANES), lambda i, j: (i, 0))],
        out_shape=[jax.ShapeDtypeStruct((t, n), BF16),
                   jax.ShapeDtypeStruct((t, V7X_LANES), F32)],
        scratch_shapes=[pltpu.VMEM((tm, d), BF16), pltpu.VMEM((8, d), F32)],
        compiler_params=_params("arbitrary", "arbitrary"),
        name="inproj",
    )(x2, g, mod3, mod3, w_all, w_gl)


def _mixer_a_kernel(u_ref, v_ref, lng_ref, lnb_ref, ws_ref, bs_ref, og_ref, o_ref, y_ref):
    tc, aw = u_ref.shape
    nh, c, _ = ws_ref.shape
    hd = aw // nh
    row = lax.broadcasted_iota(jnp.int32, (c, c), 0)
    col = lax.broadcasted_iota(jnp.int32, (c, c), 1)
    causal = col <= row
    for ci in range(tc // c):
        rows = slice(ci * c, (ci + 1) * c)
        v = v_ref[rows, :].astype(F32)
        mu = jnp.mean(v, axis=-1, keepdims=True)
        vc = v - mu
        var = jnp.mean(vc * vc, axis=-1, keepdims=True)
        vn = (vc * lax.rsqrt(var + EPS) * lng_ref[...] + lnb_ref[...]).astype(BF16)
        ssq = jnp.zeros((c, 1), F32)
        for h in range(nh):
            cols = slice(h * hd, (h + 1) * hd)
            w = jnp.where(causal, ws_ref[h], 0.0).astype(BF16)
            s = jnp.dot(w, vn[:, cols], preferred_element_type=F32) + bs_ref[:, h:h + 1]
            y = u_ref[rows, cols].astype(F32) * s
            y_ref[:, cols] = y
            ssq = ssq + jnp.sum(y * y, axis=-1, keepdims=True)
        inv = lax.rsqrt(ssq / aw + EPS)
        o_ref[rows, :] = (y_ref[...] * inv * og_ref[...]).astype(o_ref.dtype)


def _mixer_a(proj, ln_g, ln_b, w_s, b_s, out_g, aw, tc=256):
    t = proj.shape[0]
    nh, c, _ = w_s.shape
    return pl.pallas_call(
        _mixer_a_kernel,
        grid=(t // tc,),
        in_specs=[pl.BlockSpec((tc, aw), lambda i: (i, 0)),
                  pl.BlockSpec((tc, aw), lambda i: (i, 1)),
                  pl.BlockSpec((1, aw), lambda i: (0, 0)),
                  pl.BlockSpec((1, aw), lambda i: (0, 0)),
                  pl.BlockSpec((nh, c, c), lambda i: (0, 0, 0)),
                  pl.BlockSpec((c, nh), lambda i: (0, 0)),
                  pl.BlockSpec((1, aw), lambda i: (0, 0))],
        out_specs=pl.BlockSpec((tc, aw), lambda i: (i, 0)),
        out_shape=jax.ShapeDtypeStruct((t, aw), BF16),
        scratch_shapes=[pltpu.VMEM((c, aw), F32)],
        compiler_params=_params("arbitrary"),
        name="mixer_a",
    )(proj, proj, ln_g, ln_b, w_s, b_s.T, out_g)


def _split_bf16(x, terms):
    out = []
    for _ in range(terms - 1):
        hi = x.astype(BF16)
        out.append(hi)
        x = x - hi.astype(F32)
    out.append(x.astype(BF16))
    return out


_NT = (((1,), (1,)), ((), ()))
_TN = (((0,), (0,)), ((), ()))


def _mixer_b_kernel(q_ref, k_ref, v_ref, r_ref, gl_ref, wg2_ref, bg2_ref, hg_ref,
                    o_ref, st_ref, cum_ref):
    tb, kw = q_ref.shape
    vw = v_ref.shape[1]
    nh = st_ref.shape[0]
    dk, dv = kw // nh, vw // nh
    c = B_CHUNK
    dot = functools.partial(jnp.dot, preferred_element_type=F32)

    @pl.when(pl.program_id(1) == 0)
    def _():
        st_ref[...] = jnp.zeros_like(st_ref)

    gl_hi, gl_lo = _split_bf16(gl_ref[...], 2)
    logits = (dot(gl_hi, wg2_ref[0]) + dot(gl_hi, wg2_ref[1]) + dot(gl_lo, wg2_ref[0])
              + bg2_ref[...])
    log_a = jax.nn.log_sigmoid(logits) / B_GATE_TAU
    row = lax.broadcasted_iota(jnp.int32, (tb, tb), 0)
    col = lax.broadcasted_iota(jnp.int32, (tb, tb), 1)
    chunk_start = row - jnp.bitwise_and(row, c - 1)
    tri = jnp.where(col <= row, jnp.where(col >= chunk_start, 1.0, 0.0), 0.0).astype(BF16)
    cum = None
    for part in _split_bf16(log_a, 3):
        cum = dot(tri, part) if cum is None else cum + dot(tri, part)
    cum_ref[...] = cum

    crow = lax.broadcasted_iota(jnp.int32, (c, c), 0)
    ccol = lax.broadcasted_iota(jnp.int32, (c, c), 1)
    causal = ccol <= crow
    for ci in range(tb // c):
        rows = slice(ci * c, (ci + 1) * c)
        for h in range(nh):
            ks = slice(h * dk, (h + 1) * dk)
            vs = slice(h * dv, (h + 1) * dv)
            cum_h = cum_ref[rows, ks]
            last = cum_h[c - 1:c, :]
            q = q_ref[rows, ks].astype(F32) * (dk ** -0.5)
            k = k_ref[rows, ks].astype(F32)
            q_dec = (q * jnp.exp(cum_h)).astype(BF16)
            k_dec = (k * jnp.exp(-cum_h)).astype(BF16)
            k_state = (k * jnp.exp(last - cum_h)).astype(BF16)
            vh = v_ref[rows, vs]
            attn = lax.dot_general(q_dec, k_dec, _NT, preferred_element_type=F32)
            attn = jnp.where(causal, attn, 0.0).astype(BF16)
            state_t = st_ref[h]
            o = dot(attn, vh) + lax.dot_general(q_dec, state_t.astype(BF16), _NT,
                                                preferred_element_type=F32)
            kv_t = lax.dot_general(vh, k_state, _TN, preferred_element_type=F32)
            st_ref[h] = state_t * jnp.exp(last) + kv_t
            o = _rms_rows(o, hg_ref[...])
            o_ref[rows, vs] = (o * _silu(r_ref[rows, vs].astype(F32))).astype(o_ref.dtype)


def _mixer_b(proj, g_low, w_g2p, b_g2, head_g, bsz, seq, kw, vw, tb=256):
    t = proj.shape[0]
    per_b = seq // tb
    q_blk = (2 * vw) // kw
    v_blk = (2 * vw + 2 * kw) // vw
    row = lambda b, n: b * per_b + n
    return pl.pallas_call(
        _mixer_b_kernel,
        grid=(bsz, per_b),
        in_specs=[pl.BlockSpec((tb, kw), lambda b, n: (row(b, n), q_blk)),
                  pl.BlockSpec((tb, kw), lambda b, n: (row(b, n), q_blk + 1)),
                  pl.BlockSpec((tb, vw), lambda b, n: (row(b, n), v_blk)),
                  pl.BlockSpec((tb, vw), lambda b, n: (row(b, n), v_blk + 1)),
                  pl.BlockSpec((tb, V7X_LANES), lambda b, n: (row(b, n), 0)),
                  pl.BlockSpec((2, V7X_LANES, kw), lambda b, n: (0, 0, 0)),
                  pl.BlockSpec((1, kw), lambda b, n: (0, 0)),
                  pl.BlockSpec((1, vw // B_HEADS), lambda b, n: (0, 0))],
        out_specs=pl.BlockSpec((tb, vw), lambda b, n: (row(b, n), 0)),
        out_shape=jax.ShapeDtypeStruct((t, vw), BF16),
        scratch_shapes=[pltpu.VMEM((B_HEADS, vw // B_HEADS, kw // B_HEADS), F32),
                        pltpu.VMEM((tb, kw), F32)],
        compiler_params=_params("arbitrary", "arbitrary"),
        name="mixer_b",
    )(proj, proj, proj, proj, g_low, w_g2p, b_g2, head_g)


def _fold_lanes(v):
    out = v[:, :V7X_LANES]
    for q in range(1, v.shape[1] // V7X_LANES):
        out = out + v[:, q * V7X_LANES:(q + 1) * V7X_LANES]
    return out


def _inv_rms(part, d):
    return lax.rsqrt(jnp.sum(part, axis=-1, keepdims=True) / d + EPS)


def _outmm_kernel(ya_ref, yb_ref, wa_ref, wb_ref, o_ref):
    o_ref[...] = (jnp.dot(ya_ref[...], wa_ref[...], preferred_element_type=F32)
                  + jnp.dot(yb_ref[...], wb_ref[...], preferred_element_type=F32))


def _outmm(ya, yb, w_out, tm=1024, tn=1024):
    t, aw = ya.shape
    d = w_out.shape[1]
    return pl.pallas_call(
        _outmm_kernel,
        grid=(t // tm, d // tn),
        in_specs=[pl.BlockSpec((tm, aw), lambda i, j: (i, 0)),
                  pl.BlockSpec((tm, aw), lambda i, j: (i, 0)),
                  pl.BlockSpec((aw, tn), lambda i, j: (0, j)),
                  pl.BlockSpec((aw, tn), lambda i, j: (1, j))],
        out_specs=pl.BlockSpec((tm, tn), lambda i, j: (i, j)),
        out_shape=jax.ShapeDtypeStruct((t, d), F32),
        compiler_params=_params("arbitrary", "arbitrary"),
        name="outmm",
    )(ya, yb, w_out, w_out)


def _postmix_kernel(y_ref, x_ref, gpost_ref, gate_ref, gpre_ref, sc_ref, sh_ref, wr_ref,
                    x1_ref, h2b_ref, h2s_ref, lg_ref, h2lo_ref, vec_ref):
    tm, d = x_ref.shape
    nr, pitch = _slab_rows(d), _slab_pitch(d)
    vec_ref[0:1, :] = gpost_ref[...] * gate_ref[0]
    vec_ref[1:2, :] = gpre_ref[...] * (1.0 + sc_ref[0])
    vec_ref[2:3, :] = sh_ref[0]

    def rows_of(r, u):
        return pl.multiple_of((r * NORM_UNROLL + u) * NORM_ROWS, NORM_ROWS)

    def body(r, carry):
        groups = [rows_of(r, u) for u in range(NORM_UNROLL)]
        invs = []
        for r0 in groups:
            rows = pl.ds(r0, NORM_ROWS)
            part = jnp.zeros((NORM_ROWS, V7X_LANES), F32)
            for c0 in range(0, d, NORM_COLS):
                y = y_ref[rows, c0:c0 + NORM_COLS]
                part = part + _fold_lanes(y * y)
            invs.append(_inv_rms(part, d))
        invs2 = []
        for r0, inv in zip(groups, invs):
            rows = pl.ds(r0, NORM_ROWS)
            part = jnp.zeros((NORM_ROWS, V7X_LANES), F32)
            for c0 in range(0, d, NORM_COLS):
                cols = slice(c0, c0 + NORM_COLS)
                x1 = x_ref[rows, cols] + (y_ref[rows, cols] * inv) * vec_ref[0:1, cols]
                x1_ref[rows, cols] = x1
                part = part + _fold_lanes(x1 * x1)
            invs2.append(_inv_rms(part, d))
        for r0, inv in zip(groups, invs2):
            rows = pl.ds(r0, NORM_ROWS)
            for c0 in range(0, d, NORM_COLS):
                cols = slice(c0, c0 + NORM_COLS)
                h2 = (x1_ref[rows, cols] * inv) * vec_ref[1:2, cols] + vec_ref[2:3, cols]
                hb = h2.astype(BF16)
                h2b_ref[rows, cols] = hb
                h2lo_ref[rows, cols] = (h2 - hb.astype(F32)).astype(BF16)
                for q in range(NORM_COLS // V7X_LANES):
                    s = c0 // V7X_LANES + q
                    h2s_ref[pl.ds(r0 * pitch + s, NORM_ROWS, stride=pitch), :] = (
                        h2[:, q * V7X_LANES:(q + 1) * V7X_LANES])
            for s in range(nr, pitch):
                h2s_ref[pl.ds(r0 * pitch + s, NORM_ROWS, stride=pitch), :] = (
                    jnp.zeros((NORM_ROWS, V7X_LANES), F32))
        return carry

    lax.fori_loop(0, tm // (NORM_ROWS * NORM_UNROLL), body, 0)
    p = jnp.dot(h2b_ref[...], wr_ref[...], preferred_element_type=F32)
    p_lo = jnp.dot(h2lo_ref[...], wr_ref[:, :V7X_LANES], preferred_element_type=F32)
    lg_ref[...] = p[:, :V7X_LANES] + p[:, V7X_LANES:] + p_lo


def _postmix(ymix, x2, gpost, mod3, gpre, w_rp, seq, tm=256):
    t, d = x2.shape
    per_b = seq // tm
    pitch = _slab_pitch(d)
    modspec = lambda m: pl.BlockSpec((1, 1, d), lambda i: ((i // per_b) * N_MOD + m, 0, 0))
    vecspec = pl.BlockSpec((1, d), lambda i: (0, 0))
    return pl.pallas_call(
        _postmix_kernel,
        grid=(t // tm,),
        in_specs=[pl.BlockSpec((tm, d), lambda i: (i, 0)),
                  pl.BlockSpec((tm, d), lambda i: (i, 0)),
                  vecspec, modspec(2), vecspec, modspec(4), modspec(3),
                  pl.BlockSpec((d, 2 * V7X_LANES), lambda i: (0, 0))],
        out_specs=[pl.BlockSpec((tm, d), lambda i: (i, 0)),
                   pl.BlockSpec((tm, d), lambda i: (i, 0)),
                   pl.BlockSpec((tm * pitch, V7X_LANES), lambda i: (i, 0)),
                   pl.BlockSpec((tm, V7X_LANES), lambda i: (i, 0))],
        out_shape=[jax.ShapeDtypeStruct((t, d), F32),
                   jax.ShapeDtypeStruct((t, d), BF16),
                   jax.ShapeDtypeStruct((t * pitch, V7X_LANES), F32),
                   jax.ShapeDtypeStruct((t, V7X_LANES), F32)],
        scratch_shapes=[pltpu.VMEM((tm, d), BF16),
                        pltpu.VMEM((8, d), F32)],
        compiler_params=_params("arbitrary"),
        name="postmix",
    )(ymix, x2, gpost, mod3, gpre, mod3, mod3, w_rp)


def _first_argmax(x, iota, axis, size):
    m = jnp.max(x, axis=axis, keepdims=True)
    idx = jnp.min(jnp.where(x == m, iota, size), axis=axis, keepdims=True)
    return m, idx


def _route_kernel(lg_ref, bias_ref, idx_ref, w_ref, rank_ref, cnt_ref, seen_ref):
    tt = lg_ref.shape[0]
    ne, ng = N_EXPERTS, N_GROUPS

    @pl.when(pl.program_id(0) == 0)
    def _():
        seen_ref[...] = jnp.zeros_like(seen_ref)

    gs = ne // ng
    neg = -jnp.inf
    scores = jax.nn.sigmoid(lg_ref[...].T[:ne, :])
    sel = scores + bias_ref[...]
    g3 = sel.reshape(ng, gs, tt)
    j_iota = lax.broadcasted_iota(jnp.int32, (ng, gs, tt), 1)
    m1, i1 = _first_argmax(g3, j_iota, 1, gs)
    m2 = jnp.max(jnp.where(j_iota == i1, neg, g3), axis=1, keepdims=True)
    grp = (m1 + m2).reshape(ng, tt)
    g_iota = lax.broadcasted_iota(jnp.int32, (ng, tt), 0)
    keep = jnp.zeros((ng, tt), jnp.bool_)
    for _ in range(TOPK_GROUPS):
        _, gi = _first_argmax(grp, g_iota, 0, ng)
        hit = g_iota == gi
        keep = jnp.logical_or(keep, hit)
        grp = jnp.where(hit, neg, grp)
    keep3 = jnp.broadcast_to(keep.reshape(ng, 1, tt), (ng, gs, tt))
    cand = jnp.where(keep3, g3, neg).reshape(ne, tt)
    e_iota = lax.broadcasted_iota(jnp.int32, (ne, tt), 0)
    idxs, ws, hits = [], [], []
    for _ in range(TOP_K):
        _, ei = _first_argmax(cand, e_iota, 0, ne)
        hit = e_iota == ei
        idxs.append(ei)
        hits.append(hit)
        ws.append(jnp.sum(jnp.where(hit, scores, 0.0), axis=0, keepdims=True))
        cand = jnp.where(hit, neg, cand)
    total = ws[0]
    for w in ws[1:]:
        total = total + w
    pad = idx_ref.shape[0] - TOP_K
    idx_ref[...] = jnp.concatenate(idxs + [jnp.zeros((pad, tt), jnp.int32)], axis=0)
    w_ref[...] = jnp.concatenate([w / total * ROUTED_SCALE for w in ws]
                                 + [jnp.zeros((pad, tt), F32)], axis=0)

    chosen = hits[0]
    for hit in hits[1:]:
        chosen = jnp.logical_or(chosen, hit)
    chosen = jnp.where(chosen, 1.0, 0.0)
    src = lax.broadcasted_iota(jnp.int32, (tt, tt), 0)
    dst = lax.broadcasted_iota(jnp.int32, (tt, tt), 1)
    before = jnp.where(src < dst, 1.0, 0.0).astype(BF16)
    rank_e = seen_ref[:, 0:1] + jnp.dot(chosen.astype(BF16), before,
                                        preferred_element_type=F32)
    ranks = [jnp.sum(jnp.where(hit, rank_e, 0.0), axis=0, keepdims=True) for hit in hits]
    rank_ref[...] = jnp.concatenate(ranks + [jnp.zeros((pad, tt), F32)],
                                    axis=0).astype(jnp.int32)
    seen_ref[...] = seen_ref[...] + jnp.sum(chosen, axis=1, keepdims=True)
    cnt_ref[...] = seen_ref[...].astype(jnp.int32)


def _route(logits, bias, tt=512):
    t = logits.shape[0]
    tok_spec = pl.BlockSpec((8, tt), lambda i: (0, i))
    return pl.pallas_call(
        _route_kernel,
        grid=(t // tt,),
        in_specs=[pl.BlockSpec((tt, V7X_LANES), lambda i: (i, 0)),
                  pl.BlockSpec((N_EXPERTS, 1), lambda i: (0, 0))],
        out_specs=[tok_spec, tok_spec, tok_spec,
                   pl.BlockSpec((N_EXPERTS, V7X_LANES), lambda i: (0, 0))],
        out_shape=[jax.ShapeDtypeStruct((8, t), jnp.int32),
                   jax.ShapeDtypeStruct((8, t), F32),
                   jax.ShapeDtypeStruct((8, t), jnp.int32),
                   jax.ShapeDtypeStruct((N_EXPERTS, V7X_LANES), jnp.int32)],
        scratch_shapes=[pltpu.VMEM((N_EXPERTS, V7X_LANES), F32)],
        compiler_params=_params("arbitrary"),
        name="route",
    )(logits, bias.reshape(N_EXPERTS, 1))


def _plan_kernel(idx_ref, rank_ref, cnt_ref, dest_ref, be_ref, nr_ref, tgt_ref, ws_ref,
                 *, bm, nch):
    ne = N_EXPERTS
    shift = bm.bit_length() - 1
    counts = cnt_ref[...]
    padded = ((counts + (bm - 1)) >> shift) << shift
    r = lax.broadcasted_iota(jnp.int32, (ne, ne), 0)
    c = lax.broadcasted_iota(jnp.int32, (ne, ne), 1)
    upto = jnp.where(c <= r, 1.0, 0.0)
    pend = jnp.dot(upto, padded.astype(F32), precision=HIGHEST,
                   preferred_element_type=F32).astype(jnp.int32)
    pstart = pend - padded
    idx = idx_ref[...]
    dest = rank_ref[...]
    blk = lax.broadcasted_iota(jnp.int32, be_ref.shape, 1)
    blk_row0 = blk * bm
    blk_e = jnp.zeros(be_ref.shape, jnp.int32)
    lane_e = lax.broadcasted_iota(jnp.int32, ws_ref.shape, 1)
    used_before = jnp.zeros(ws_ref.shape, jnp.int32)
    for e in range(ne):
        dest = dest + jnp.where(idx == e, pstart[e:e + 1, 0:1], 0)
        blk_e = blk_e + jnp.where(pend[e:e + 1, 0:1] <= blk_row0, 1, 0)
        used_before = used_before + jnp.where(
            lane_e > e, jnp.where(counts[e:e + 1, 0:1] > 0, 1, 0), 0)
    dest_ref[...] = dest
    blk_e = jnp.minimum(blk_e, ne - 1)
    be_ref[...] = blk_e
    nr_ref[...] = jnp.broadcast_to(pend[ne - 1:ne, :] >> shift, nr_ref.shape)
    ws_ref[...] = jnp.bitwise_and(used_before, 1)

    seg_end = jnp.zeros(be_ref.shape, jnp.int32)
    seg_start = jnp.zeros(be_ref.shape, jnp.int32)
    for e in range(ne):
        mine = blk_e == e
        seg_end = seg_end + jnp.where(mine, pend[e:e + 1, 0:1], 0)
        seg_start = seg_start + jnp.where(mine, pstart[e:e + 1, 0:1], 0)
    nxt = jnp.zeros(be_ref.shape, jnp.int32)
    for e in range(ne):
        nxt = nxt + jnp.where(pend[e:e + 1, 0:1] <= seg_end, 1, 0)
    need = jnp.where(nxt < ne, (nxt - blk_e) * nch, 0)
    n_blk = jnp.maximum((seg_end - seg_start) >> shift, 1)
    j = blk - (seg_start >> shift)
    per = (need.astype(F32) / n_blk.astype(F32)).astype(jnp.int32)
    per = per + jnp.where((per + 1) * n_blk <= need, 1, 0)
    per = per - jnp.where(per * n_blk > need, 1, 0)
    per = per + jnp.where(per * n_blk < need, 1, 0)
    tgt_ref[...] = (blk_e + 1) * nch + jnp.minimum(need, (j + 1) * per)


def _plan(idx8, rank8, cnt, bm, nb, nch, tt=2048):
    t = idx8.shape[1]
    tt = min(tt, t)
    nbp = -(-nb // V7X_LANES) * V7X_LANES
    tok_spec = pl.BlockSpec((8, tt), lambda i: (0, i))
    blk_spec = pl.BlockSpec((8, nbp), lambda i: (0, 0))
    one_spec = pl.BlockSpec((8, V7X_LANES), lambda i: (0, 0))
    return pl.pallas_call(
        functools.partial(_plan_kernel, bm=bm, nch=nch),
        grid=(t // tt,),
        in_specs=[tok_spec, tok_spec,
                  pl.BlockSpec((N_EXPERTS, V7X_LANES), lambda i: (0, 0))],
        out_specs=[tok_spec, blk_spec, one_spec, blk_spec, one_spec],
        out_shape=[jax.ShapeDtypeStruct((8, t), jnp.int32),
                   jax.ShapeDtypeStruct((8, nbp), jnp.int32),
                   jax.ShapeDtypeStruct((8, V7X_LANES), jnp.int32),
                   jax.ShapeDtypeStruct((8, nbp), jnp.int32),
                   jax.ShapeDtypeStruct((8, V7X_LANES), jnp.int32)],
        compiler_params=_params("arbitrary"),
        name="plan",
    )(idx8, rank8, cnt)


def _slots_kernel(cnt_ref, dest_ref, st_ref, *, bm):
    i = pl.program_id(0)
    tt = dest_ref.shape[1]
    ns = st_ref.shape[0]

    @pl.when(i == 0)
    def _():
        def clear(lo, hi):
            def body(s, carry):
                st_ref[s] = 0
                return carry
            lax.fori_loop(lo, hi, body, 0)

        def per_expert(e, seg_start):
            n = cnt_ref[e]
            seg_end = seg_start + (n + (bm - 1)) // bm * bm
            clear(seg_start + n, seg_end)
            return seg_end

        used = lax.fori_loop(0, N_EXPERTS, per_expert, 0)
        clear(used, ns)

    def body(t, carry):
        for k in range(TOP_K):
            st_ref[dest_ref[k, t]] = i * tt + t
        return carry

    lax.fori_loop(0, tt, body, 0, unroll=8)


def _slots(counts, dest8, bm, nb, tt=2048):
    t = dest8.shape[1]
    tt = min(tt, t)
    return pl.pallas_call(
        functools.partial(_slots_kernel, bm=bm),
        grid=(t // tt,),
        in_specs=[pl.BlockSpec(memory_space=pltpu.SMEM),
                  pl.BlockSpec((8, tt), lambda i: (0, i), memory_space=pltpu.SMEM)],
        out_specs=pl.BlockSpec(memory_space=pltpu.SMEM),
        out_shape=jax.ShapeDtypeStruct((nb * bm,), jnp.int32),
        compiler_params=_params("arbitrary"),
        name="slots",
    )(counts, dest8)


def _row_gather(src_hbm, dst, sem, src_tok, dst_tok, nr, pitch):
    return pltpu.make_async_copy(src_hbm.at[pl.ds(src_tok * pitch, nr)],
                                 dst.at[pl.ds(dst_tok * pitch, nr)], sem)


def _gather_wait(src_hbm, dst, sem, n_tok, nr):
    pltpu.make_async_copy(src_hbm.at[pl.ds(0, n_tok * nr)],
                          dst.at[pl.ds(0, n_tok * nr)], sem).wait()


WCHUNK_ROWS = 512


def _wchunks(d):
    each = d // WCHUNK_ROWS
    return each, 3 * each


def _experts_kernel(be_ref, nr_ref, st_ref, tgt_ref, ws_ref,
                    h2s_hbm, wg_hbm, wu_hbm, wd_hbm, y_ref,
                    xbuf, sem, wg_b, wu_b, wd_b, stg_a, stg_b, wsem, done_ref):
    i = pl.program_id(0)
    n_real = nr_ref[0]
    _, d, f = wg_b.shape
    nr, pitch = _slab_rows(d), _slab_pitch(d)
    bm = xbuf.shape[1] // pitch
    each, nch = _wchunks(d)
    dn_rows = f // each
    slot = lax.rem(i, 2)
    e_cur = be_ref[jnp.maximum(jnp.minimum(i, n_real - 1), 0)]
    limit = (be_ref[jnp.maximum(n_real - 1, 0)] + 1) * nch

    def start_gather(blk, sl):
        def body(r, carry):
            tok = st_ref[blk * bm + r]
            _row_gather(h2s_hbm, xbuf.at[sl], sem.at[sl], tok, r, nr, pitch).start()
            return carry
        lax.fori_loop(0, bm, body, 0, unroll=8)

    def chunk_parts(g):
        ex = g // nch
        c = g - ex * nch
        return ex, c

    def chunk_rows(kind, c):
        n = dn_rows if kind == 2 else WCHUNK_ROWS
        return pl.ds(lax.rem(c, each) * n, n)

    def chunk_copy(kind, ex, c, sl):
        src = (wg_hbm, wu_hbm, wd_hbm)[kind]
        stg = stg_b if kind == 2 else stg_a
        return pltpu.make_async_copy(src.at[ex, chunk_rows(kind, c), :], stg.at[sl],
                                     wsem.at[sl])

    def chunk_start(g, sl):
        ex, c = chunk_parts(g)
        for kind in range(3):
            @pl.when(c // each == kind)
            def _():
                chunk_copy(kind, ex, c, sl).start()

    def chunk_finish(g, sl):
        ex, c = chunk_parts(g)
        wsl = ws_ref[ex]
        for kind in range(3):
            @pl.when(c // each == kind)
            def _():
                chunk_copy(kind, ex, c, sl).wait()
                stg = stg_b if kind == 2 else stg_a
                dst = (wg_b, wu_b, wd_b)[kind]
                dst[wsl, chunk_rows(kind, c), :] = stg[sl].astype(BF16)

    def convert_until(target):
        def body(g, carry):
            sl = lax.rem(g, 2)
            chunk_finish(g, sl)

            @pl.when(g + 2 < limit)
            def _():
                chunk_start(g + 2, sl)
            return carry
        lax.fori_loop(done_ref[0], target, body, 0)
        done_ref[0] = jnp.maximum(done_ref[0], target)

    @pl.when(i == 0)
    def _():
        done_ref[0] = 0
        chunk_start(0, 0)
        chunk_start(1, 1)
        start_gather(0, 0)

    @pl.when(i + 1 < n_real)
    def _():
        start_gather(i + 1, 1 - slot)

    @pl.when(i < n_real)
    def _():
        convert_until((e_cur + 1) * nch)
        wsl = ws_ref[e_cur]
        _gather_wait(h2s_hbm, xbuf.at[slot], sem.at[slot], bm, nr)
        g = jnp.zeros((bm, f), F32)
        u = jnp.zeros((bm, f), F32)
        for kc in range(nr // SLAB_GROUP):
            xs = _slab_cols(xbuf.at[slot], 0, bm, pitch, kc).astype(BF16)
            ks = slice(kc * SLAB_K, (kc + 1) * SLAB_K)
            g = g + jnp.dot(xs, wg_b[wsl, ks, :], preferred_element_type=F32)
            u = u + jnp.dot(xs, wu_b[wsl, ks, :], preferred_element_type=F32)
        a = (_silu(g) * u).astype(BF16)
        for kc in range(nr // SLAB_GROUP):
            y = jnp.dot(a, wd_b[wsl, :, kc * SLAB_K:(kc + 1) * SLAB_K],
                        preferred_element_type=F32)
            for q in range(SLAB_GROUP):
                y_ref[pl.ds(kc * SLAB_GROUP + q, bm, stride=pitch), :] = (
                    y[:, q * V7X_LANES:(q + 1) * V7X_LANES])
        for s in range(nr, pitch):
            y_ref[pl.ds(s, bm, stride=pitch), :] = jnp.zeros((bm, V7X_LANES), F32)
        convert_until(tgt_ref[i])

    @pl.when(i >= n_real)
    def _():
        y_ref[...] = jnp.zeros_like(y_ref)


def _experts(block_e, n_real, slot_tok, tgt, wslot, h2s, wg, wu, wd, bm):
    nb = block_e.shape[0]
    ne, d, f = wg.shape
    pitch = _slab_pitch(d)
    each, _ = _wchunks(d)
    hbm = pl.BlockSpec(memory_space=pl.ANY)
    return pl.pallas_call(
        _experts_kernel,
        grid_spec=pltpu.PrefetchScalarGridSpec(
            num_scalar_prefetch=5,
            grid=(nb,),
            in_specs=[hbm, hbm, hbm, hbm],
            out_specs=pl.BlockSpec((bm * pitch, V7X_LANES), lambda i, *_: (i, 0)),
            scratch_shapes=[pltpu.VMEM((2, bm * pitch, V7X_LANES), F32),
                            pltpu.SemaphoreType.DMA((2,)),
                            pltpu.VMEM((2, d, f), BF16),
                            pltpu.VMEM((2, d, f), BF16),
                            pltpu.VMEM((2, f, d), BF16),
                            pltpu.VMEM((2, WCHUNK_ROWS, f), F32),
                            pltpu.VMEM((2, f // each, d), F32),
                            pltpu.SemaphoreType.DMA((2,)),
                            pltpu.SMEM((1,), jnp.int32)]),
        out_shape=jax.ShapeDtypeStruct((nb * bm * pitch, V7X_LANES), F32),
        compiler_params=_params("arbitrary"),
        name="experts",
    )(block_e, n_real, slot_tok, tgt, wslot, h2s, wg, wu, wd)


def _shared_kernel(h_ref, wg_ref, wu_ref, wd_ref, o_ref):
    h = h_ref[...]
    g = jnp.dot(h, wg_ref[...], preferred_element_type=F32)
    u = jnp.dot(h, wu_ref[...], preferred_element_type=F32)
    a = (_silu(g) * u).astype(BF16)
    o_ref[...] = jnp.dot(a, wd_ref[...], preferred_element_type=F32)


def _shared(h2b, wg, wu, wd, tm=256):
    t, d = h2b.shape
    f = wg.shape[1]
    return pl.pallas_call(
        _shared_kernel,
        grid=(t // tm,),
        in_specs=[pl.BlockSpec((tm, d), lambda i: (i, 0)),
                  pl.BlockSpec((d, f), lambda i: (0, 0)),
                  pl.BlockSpec((d, f), lambda i: (0, 0)),
                  pl.BlockSpec((f, d), lambda i: (0, 0))],
        out_specs=pl.BlockSpec((tm, d), lambda i: (i, 0)),
        out_shape=jax.ShapeDtypeStruct((t, d), F32),
        compiler_params=_params("arbitrary"),
        name="shared",
    )(h2b, wg, wu, wd)


def _combine_kernel(dest_ref, ys_hbm, w_ref, sh_ref, x1_ref, g_ref, gate_ref,
                    o_ref, gbuf, ybuf, vec_ref, sem):
    i = pl.program_id(0)
    n = pl.num_programs(0)
    tt, d = x1_ref.shape
    nr, pitch = _slab_rows(d), _slab_pitch(d)
    slot = lax.rem(i, 2)

    def start_gather(blk, sl):
        def body(r, carry):
            for k in range(TOP_K):
                row = dest_ref[k * (n * tt) + blk * tt + r]
                _row_gather(ys_hbm, gbuf.at[sl], sem.at[sl], row, k * tt + r,
                            nr, pitch).start()
            return carry
        lax.fori_loop(0, tt, body, 0, unroll=4)

    @pl.when(i == 0)
    def _():
        start_gather(0, 0)

    @pl.when(i + 1 < n)
    def _():
        start_gather(i + 1, 1 - slot)

    _gather_wait(ys_hbm, gbuf.at[slot], sem.at[slot], TOP_K * tt, nr)
    wk = [jnp.broadcast_to(w_ref[:, k:k + 1], (tt, V7X_LANES)) for k in range(TOP_K)]
    for s in range(nr):
        cols = slice(s * V7X_LANES, (s + 1) * V7X_LANES)
        acc = sh_ref[:, cols]
        for k in range(TOP_K):
            rows = gbuf[slot, pl.ds(k * tt * pitch + s, tt, stride=pitch), :]
            acc = acc + rows * wk[k]
        ybuf[:, cols] = acc
    vec_ref[0:1, :] = g_ref[...] * gate_ref[0]

    def body(r, carry):
        groups = [pl.ds(pl.multiple_of((r * NORM_UNROLL + u) * NORM_ROWS, NORM_ROWS),
                        NORM_ROWS) for u in range(NORM_UNROLL)]
        invs = []
        for rows in groups:
            part = jnp.zeros((NORM_ROWS, V7X_LANES), F32)
            for c0 in range(0, d, NORM_COLS):
                yc = ybuf[rows, c0:c0 + NORM_COLS]
                part = part + _fold_lanes(yc * yc)
            invs.append(_inv_rms(part, d))
        for rows, inv in zip(groups, invs):
            for c0 in range(0, d, NORM_COLS):
                cols = slice(c0, c0 + NORM_COLS)
                o_ref[rows, cols] = (x1_ref[rows, cols]
                                     + (ybuf[rows, cols] * inv) * vec_ref[0:1, cols])
        return carry

    lax.fori_loop(0, tt // (NORM_ROWS * NORM_UNROLL), body, 0)


def _combine(dest, y_rows, wts, shared, x1, g, mod3, seq, tt=128):
    t, d = x1.shape
    pitch = _slab_pitch(d)
    per_b = seq // tt
    return pl.pallas_call(
        _combine_kernel,
        grid_spec=pltpu.PrefetchScalarGridSpec(
            num_scalar_prefetch=1,
            grid=(t // tt,),
            in_specs=[pl.BlockSpec(memory_space=pl.ANY),
                      pl.BlockSpec((tt, 8), lambda i, ds: (i, 0)),
                      pl.BlockSpec((tt, d), lambda i, ds: (i, 0)),
                      pl.BlockSpec((tt, d), lambda i, ds: (i, 0)),
                      pl.BlockSpec((1, d), lambda i, ds: (0, 0)),
                      pl.BlockSpec((1, 1, d), lambda i, ds: ((i // per_b) * N_MOD + 5, 0, 0))],
            out_specs=pl.BlockSpec((tt, d), lambda i, ds: (i, 0)),
            scratch_shapes=[pltpu.VMEM((2, TOP_K * tt * pitch, V7X_LANES), F32),
                            pltpu.VMEM((tt, d), F32),
                            pltpu.VMEM((8, d), F32),
                            pltpu.SemaphoreType.DMA((2,))]),
        out_shape=jax.ShapeDtypeStruct((t, d), F32),
        compiler_params=_params("arbitrary"),
        name="combine",
    )(dest, y_rows, wts, shared, x1, g, mod3)


EXPERT_BLOCK_ROWS = 128


def kernel(x, c, w_ada, b_ada, g_pre_mix, g_post_mix, g_pre_ffn, g_post_ffn, w_in,
           a_ln_g, a_ln_b, a_w_s, a_b_s, a_out_g, b_w_g2, b_b_g2, b_head_g, w_out,
           w_router, router_bias, we_gate, we_up, we_down, ws_gate, ws_up, ws_down):
    bsz, seq, d = x.shape
    t = bsz * seq
    aw = d // 2
    vw = d - aw
    kw = vw // 2
    n_main = 2 * aw + 2 * kw + 2 * vw
    x2 = x.reshape(t, d)
    for l in range(w_ada.shape[0]):
        mod = _ada(c, w_ada[l], b_ada[l])
        mod3 = mod.reshape(bsz * N_MOD, 1, d)

        w_gl = jnp.pad(w_in[l][:, n_main:], ((0, 0), (0, V7X_LANES - B_GATE_RANK))).astype(BF16)
        proj, g_low = _inproj(x2, g_pre_mix[l][None], mod3, w_in[l].astype(BF16), n_main,
                              w_gl, seq)
        y_a = _mixer_a(proj, a_ln_g[l][None], a_ln_b[l][None], a_w_s[l], a_b_s[l],
                       a_out_g[l][None], aw)
        w_g2p = jnp.pad(b_w_g2[l], ((0, V7X_LANES - B_GATE_RANK), (0, 0)))
        w_g2_hi = w_g2p.astype(BF16)
        w_g2p = jnp.stack([w_g2_hi, (w_g2p - w_g2_hi.astype(F32)).astype(BF16)])
        y_b = _mixer_b(proj, g_low, w_g2p, b_b_g2[l][None], b_head_g[l][None],
                       bsz, seq, kw, vw)

        w_rp = jnp.pad(w_router[l], ((0, 0), (0, V7X_LANES - N_EXPERTS)))
        w_r_hi = w_rp.astype(BF16)
        w_r_lo = (w_rp - w_r_hi.astype(F32)).astype(BF16)
        w_rp = jnp.concatenate([w_r_hi, w_r_lo], axis=1)
        ymix = _outmm(y_a, y_b, w_out[l].astype(BF16))
        x1, h2b, h2s, logits = _postmix(ymix, x2, g_post_mix[l][None], mod3,
                                        g_pre_ffn[l][None], w_rp, seq)
        idx8, w8, rank8, cnt = _route(logits, router_bias[l])
        bm = EXPERT_BLOCK_ROWS
        nb = -(-t * TOP_K // bm) + N_EXPERTS
        dest8, be8, nr8, tgt8, ws8 = _plan(idx8, rank8, cnt, bm, nb, _wchunks(d)[1])
        slot_tok = _slots(cnt[:, 0], dest8, bm, nb)
        y_rows = _experts(be8[0, :nb], nr8[0, :1], slot_tok, tgt8[0, :nb], ws8[0, :N_EXPERTS],
                          h2s, we_gate[l], we_up[l], we_down[l], bm)
        shared = _shared(h2b, ws_gate[l].astype(BF16), ws_up[l].astype(BF16),
                         ws_down[l].astype(BF16))
        x2 = _combine(dest8[:TOP_K].reshape(-1), y_rows, w8.T, shared, x1, g_post_ffn[l][None],
                      mod3, seq)
    return x2.reshape(bsz, seq, d)
```

```python
import functools

import jax
import jax.numpy as jnp
from jax import lax
from jax.experimental import pallas as pl
from jax.experimental.pallas import tpu as pltpu

F32 = jnp.float32
BF16 = jnp.bfloat16
HIGHEST = lax.Precision.HIGHEST

V7X_LANES = 128
V7X_SUBLANES = 8
V7X_VMEM_LIMIT_BYTES = 58 * 1024 * 1024

A_HEADS = 8
A_CHUNK = 128
B_HEADS = 4
B_GATE_RANK = 16
B_GATE_TAU = 16.0
B_CHUNK = 64
N_EXPERTS = 64
TOP_K = 6
N_GROUPS = 8
TOPK_GROUPS = 4
ROUTED_SCALE = 2.5
N_MOD = 6
EPS = 1e-6

SLAB_PAD = 4
U32 = jnp.uint32
HIGH_HALF = 0xFFFF0000


def _slab_rows(d):
    return d // (2 * V7X_LANES)


def _slab_pitch(d):
    return _slab_rows(d) + SLAB_PAD


def _round_bf16(x):
    return x.astype(BF16).astype(F32)


def _pack_pair(lo, hi):
    lo_bits = lax.bitcast_convert_type(lo, U32)
    hi_bits = lax.bitcast_convert_type(hi, U32)
    return jnp.bitwise_or(jnp.bitwise_and(hi_bits, U32(HIGH_HALF)),
                          jnp.right_shift(lo_bits, U32(16)))


def _unpack_pair(w):
    lo = lax.bitcast_convert_type(jnp.left_shift(w, U32(16)), F32)
    hi = lax.bitcast_convert_type(jnp.bitwise_and(w, U32(HIGH_HALF)), F32)
    return lo, hi


SLAB_GROUP = 4
SLAB_K = SLAB_GROUP * V7X_LANES


def _slab_cols(ref, first_tok, n_tok, pitch, kc):
    return jnp.concatenate(
        [ref[pl.ds(first_tok * pitch + kc * SLAB_GROUP + q, n_tok, stride=pitch), :]
         for q in range(SLAB_GROUP)], axis=-1)


def _params(*sem):
    return pltpu.CompilerParams(dimension_semantics=sem,
                                vmem_limit_bytes=V7X_VMEM_LIMIT_BYTES)


def _silu(x):
    return x * jax.nn.sigmoid(x)


ADA_ROWS = 64


def _ada_kernel(c_ref, w_ref, b_ref, o_ref, cb_ref):
    d, nb = c_ref.shape
    tn = w_ref.shape[1]

    @pl.when(pl.program_id(0) == 0)
    def _():
        def fill(i, carry):
            rows = pl.ds(pl.multiple_of(i * ADA_ROWS, ADA_ROWS), ADA_ROWS)
            cc = _silu(c_ref[rows, :])
            for b in range(nb):
                cb_ref[b, rows, :] = jnp.broadcast_to(cc[:, b:b + 1], (ADA_ROWS, V7X_LANES))
            return carry
        lax.fori_loop(0, d // ADA_ROWS, fill, 0)

    def body(i, accs):
        r0 = pl.multiple_of(i * ADA_ROWS, ADA_ROWS)
        w = w_ref[pl.ds(r0, ADA_ROWS), :]
        out = []
        for b in range(nb):
            cb = cb_ref[b, pl.ds(r0, ADA_ROWS), :]
            cols = []
            for q in range(tn // V7X_LANES):
                p = w[:, q * V7X_LANES:(q + 1) * V7X_LANES] * cb
                cols.append(p.reshape(ADA_ROWS // 8, 8, V7X_LANES).sum(axis=0))
            out.append(accs[b] + jnp.concatenate(cols, axis=-1))
        return tuple(out)

    accs = lax.fori_loop(0, d // ADA_ROWS, body,
                         tuple(jnp.zeros((8, tn), F32) for _ in range(nb)))
    for b in range(nb):
        o_ref[b:b + 1, :] = accs[b].sum(axis=0, keepdims=True) + b_ref[...]


def _ada(c, w_ada, b_ada, tn=1024):
    nb, d = c.shape
    n = w_ada.shape[1]
    return pl.pallas_call(
        _ada_kernel,
        grid=(n // tn,),
        in_specs=[pl.BlockSpec((d, nb), lambda j: (0, 0)),
                  pl.BlockSpec((d, tn), lambda j: (0, j)),
                  pl.BlockSpec((1, tn), lambda j: (0, j))],
        out_specs=pl.BlockSpec((nb, tn), lambda j: (0, j)),
        out_shape=jax.ShapeDtypeStruct((nb, n), F32),
        scratch_shapes=[pltpu.VMEM((nb, d, V7X_LANES), F32)],
        compiler_params=_params("arbitrary"),
        name="ada",
    )(c.T, w_ada, b_ada.reshape(1, n))


NORM_ROWS = 16
NORM_UNROLL = 2


def _rms_rows(x, g):
    ms = jnp.mean(x * x, axis=-1, keepdims=True)
    return x * lax.rsqrt(ms + EPS) * g


NORM_COLS = 512


def _inproj_kernel(x_ref, g_ref, sc_ref, sh_ref, w_ref, wgl_ref, o_ref, gl_ref, h_ref,
                   vec_ref):
    tm, d = x_ref.shape

    @pl.when(pl.program_id(1) == 0)
    def _():
        vec_ref[0:1, :] = g_ref[...] * (1.0 + sc_ref[0])
        vec_ref[1:2, :] = sh_ref[0]

        def body(r, carry):
            groups = [pl.ds(pl.multiple_of((r * NORM_UNROLL + u) * NORM_ROWS, NORM_ROWS),
                            NORM_ROWS) for u in range(NORM_UNROLL)]
            invs = []
            for rows in groups:
                part = jnp.zeros((NORM_ROWS, V7X_LANES), F32)
                for c0 in range(0, d, NORM_COLS):
                    xc = x_ref[rows, c0:c0 + NORM_COLS]
                    part = part + _fold_lanes(xc * xc)
                invs.append(_inv_rms(part, d))
            for rows, inv in zip(groups, invs):
                for c0 in range(0, d, NORM_COLS):
                    cols = slice(c0, c0 + NORM_COLS)
                    h = (x_ref[rows, cols] * inv) * vec_ref[0:1, cols] + vec_ref[1:2, cols]
                    h_ref[rows, cols] = h.astype(BF16)
            return carry

        lax.fori_loop(0, tm // (NORM_ROWS * NORM_UNROLL), body, 0)
        gl_ref[...] = jnp.dot(h_ref[...], wgl_ref[...], preferred_element_type=F32)

    o_ref[...] = jnp.dot(h_ref[...], w_ref[...],
                         preferred_element_type=F32).astype(o_ref.dtype)


def _inproj(x2, g, mod3, w_all, n, w_gl, seq, tm=512, tn=1024):
    t, d = x2.shape
    per_b = seq // tm
    return pl.pallas_call(
        _inproj_kernel,
        grid=(t // tm, n // tn),
        in_specs=[pl.BlockSpec((tm, d), lambda i, j: (i, 0)),
                  pl.BlockSpec((1, d), lambda i, j: (0, 0)),
                  pl.BlockSpec((1, 1, d), lambda i, j: ((i // per_b) * N_MOD + 1, 0, 0)),
                  pl.BlockSpec((1, 1, d), lambda i, j: ((i // per_b) * N_MOD + 0, 0, 0)),
                  pl.BlockSpec((d, tn), lambda i, j: (0, j)),
                  pl.BlockSpec((d, V7X_LANES), lambda i, j: (0, 0))],
        out_specs=[pl.BlockSpec((tm, tn), lambda i, j: (i, j)),
                   pl.BlockSpec((tm, V7X_LANES), lambda i, j: (i, 0))],
        out_shape=[jax.ShapeDtypeStruct((t, n), BF16),
                   jax.ShapeDtypeStruct((t, V7X_LANES), F32)],
        scratch_shapes=[pltpu.VMEM((tm, d), BF16), pltpu.VMEM((8, d), F32)],
        compiler_params=_params("arbitrary", "arbitrary"),
        name="inproj",
    )(x2, g, mod3, mod3, w_all, w_gl)


def _mixer_a_kernel(u_ref, v_ref, lng_ref, lnb_ref, ws_ref, bs_ref, og_ref, o_ref, y_ref):
    tc, aw = u_ref.shape
    nh, c, _ = ws_ref.shape
    hd = aw // nh
    row = lax.broadcasted_iota(jnp.int32, (c, c), 0)
    col = lax.broadcasted_iota(jnp.int32, (c, c), 1)
    causal = col <= row
    for ci in range(tc // c):
        rows = slice(ci * c, (ci + 1) * c)
        v = v_ref[rows, :].astype(F32)
        mu = jnp.mean(v, axis=-1, keepdims=True)
        vc = v - mu
        var = jnp.mean(vc * vc, axis=-1, keepdims=True)
        vn = (vc * lax.rsqrt(var + EPS) * lng_ref[...] + lnb_ref[...]).astype(BF16)
        ssq = jnp.zeros((c, 1), F32)
        for h in range(nh):
            cols = slice(h * hd, (h + 1) * hd)
            w = jnp.where(causal, ws_ref[h], 0.0).astype(BF16)
            s = jnp.dot(w, vn[:, cols], preferred_element_type=F32) + bs_ref[:, h:h + 1]
            y = u_ref[rows, cols].astype(F32) * s
            y_ref[:, cols] = y
            ssq = ssq + jnp.sum(y * y, axis=-1, keepdims=True)
        inv = lax.rsqrt(ssq / aw + EPS)
        o_ref[rows, :] = (y_ref[...] * inv * og_ref[...]).astype(o_ref.dtype)


def _mixer_a(proj, ln_g, ln_b, w_s, b_s, out_g, aw, tc=256):
    t = proj.shape[0]
    nh, c, _ = w_s.shape
    return pl.pallas_call(
        _mixer_a_kernel,
        grid=(t // tc,),
        in_specs=[pl.BlockSpec((tc, aw), lambda i: (i, 0)),
                  pl.BlockSpec((tc, aw), lambda i: (i, 1)),
                  pl.BlockSpec((1, aw), lambda i: (0, 0)),
                  pl.BlockSpec((1, aw), lambda i: (0, 0)),
                  pl.BlockSpec((nh, c, c), lambda i: (0, 0, 0)),
                  pl.BlockSpec((c, nh), lambda i: (0, 0)),
                  pl.BlockSpec((1, aw), lambda i: (0, 0))],
        out_specs=pl.BlockSpec((tc, aw), lambda i: (i, 0)),
        out_shape=jax.ShapeDtypeStruct((t, aw), BF16),
        scratch_shapes=[pltpu.VMEM((c, aw), F32)],
        compiler_params=_params("arbitrary"),
        name="mixer_a",
    )(proj, proj, ln_g, ln_b, w_s, b_s.T, out_g)


def _split_bf16(x, terms):
    out = []
    for _ in range(terms - 1):
        hi = x.astype(BF16)
        out.append(hi)
        x = x - hi.astype(F32)
    out.append(x.astype(BF16))
    return out


_NT = (((1,), (1,)), ((), ()))
_TN = (((0,), (0,)), ((), ()))


def _mixer_b_kernel(q_ref, k_ref, v_ref, r_ref, gl_ref, wg2_ref, bg2_ref, hg_ref,
                    o_ref, st_ref, cum_ref):
    tb, kw = q_ref.shape
    vw = v_ref.shape[1]
    nh = st_ref.shape[0]
    dk, dv = kw // nh, vw // nh
    c = B_CHUNK
    dot = functools.partial(jnp.dot, preferred_element_type=F32)

    @pl.when(pl.program_id(1) == 0)
    def _():
        st_ref[...] = jnp.zeros_like(st_ref)

    gl_hi, gl_lo = _split_bf16(gl_ref[...], 2)
    w_hi, w_lo = _split_bf16(wg2_ref[...], 2)
    logits = dot(gl_hi, w_hi) + dot(gl_hi, w_lo) + dot(gl_lo, w_hi) + bg2_ref[...]
    log_a = jax.nn.log_sigmoid(logits) / B_GATE_TAU
    row = lax.broadcasted_iota(jnp.int32, (tb, tb), 0)
    col = lax.broadcasted_iota(jnp.int32, (tb, tb), 1)
    chunk_start = row - jnp.bitwise_and(row, c - 1)
    tri = jnp.where(col <= row, jnp.where(col >= chunk_start, 1.0, 0.0), 0.0).astype(BF16)
    cum = None
    for part in _split_bf16(log_a, 3):
        cum = dot(tri, part) if cum is None else cum + dot(tri, part)
    cum_ref[...] = cum

    crow = lax.broadcasted_iota(jnp.int32, (c, c), 0)
    ccol = lax.broadcasted_iota(jnp.int32, (c, c), 1)
    causal = ccol <= crow
    for ci in range(tb // c):
        rows = slice(ci * c, (ci + 1) * c)
        for h in range(nh):
            ks = slice(h * dk, (h + 1) * dk)
            vs = slice(h * dv, (h + 1) * dv)
            cum_h = cum_ref[rows, ks]
            last = cum_h[c - 1:c, :]
            q = q_ref[rows, ks].astype(F32) * (dk ** -0.5)
            k = k_ref[rows, ks].astype(F32)
            q_dec = (q * jnp.exp(cum_h)).astype(BF16)
            k_dec = (k * jnp.exp(-cum_h)).astype(BF16)
            k_state = (k * jnp.exp(last - cum_h)).astype(BF16)
            vh = v_ref[rows, vs]
            attn = lax.dot_general(q_dec, k_dec, _NT, preferred_element_type=F32)
            attn = jnp.where(causal, attn, 0.0).astype(BF16)
            state_t = st_ref[h]
            o = dot(attn, vh) + lax.dot_general(q_dec, state_t.astype(BF16), _NT,
                                                preferred_element_type=F32)
            kv_t = lax.dot_general(vh, k_state, _TN, preferred_element_type=F32)
            st_ref[h] = state_t * jnp.exp(last) + kv_t
            o = _rms_rows(o, hg_ref[...])
            o_ref[rows, vs] = (o * _silu(r_ref[rows, vs].astype(F32))).astype(o_ref.dtype)


def _mixer_b(proj, g_low, w_g2p, b_g2, head_g, bsz, seq, kw, vw, tb=256):
    t = proj.shape[0]
    per_b = seq // tb
    q_blk = (2 * vw) // kw
    v_blk = (2 * vw + 2 * kw) // vw
    row = lambda b, n: b * per_b + n
    return pl.pallas_call(
        _mixer_b_kernel,
        grid=(bsz, per_b),
        in_specs=[pl.BlockSpec((tb, kw), lambda b, n: (row(b, n), q_blk)),
                  pl.BlockSpec((tb, kw), lambda b, n: (row(b, n), q_blk + 1)),
                  pl.BlockSpec((tb, vw), lambda b, n: (row(b, n), v_blk)),
                  pl.BlockSpec((tb, vw), lambda b, n: (row(b, n), v_blk + 1)),
                  pl.BlockSpec((tb, V7X_LANES), lambda b, n: (row(b, n), 0)),
                  pl.BlockSpec((V7X_LANES, kw), lambda b, n: (0, 0)),
                  pl.BlockSpec((1, kw), lambda b, n: (0, 0)),
                  pl.BlockSpec((1, vw // B_HEADS), lambda b, n: (0, 0))],
        out_specs=pl.BlockSpec((tb, vw), lambda b, n: (row(b, n), 0)),
        out_shape=jax.ShapeDtypeStruct((t, vw), BF16),
        scratch_shapes=[pltpu.VMEM((B_HEADS, vw // B_HEADS, kw // B_HEADS), F32),
                        pltpu.VMEM((tb, kw), F32)],
        compiler_params=_params("arbitrary", "arbitrary"),
        name="mixer_b",
    )(proj, proj, proj, proj, g_low, w_g2p, b_g2, head_g)


def _fold_lanes(v):
    out = v[:, :V7X_LANES]
    for q in range(1, v.shape[1] // V7X_LANES):
        out = out + v[:, q * V7X_LANES:(q + 1) * V7X_LANES]
    return out


def _inv_rms(part, d):
    return lax.rsqrt(jnp.sum(part, axis=-1, keepdims=True) / d + EPS)


def _outmm_kernel(ya_ref, yb_ref, wa_ref, wb_ref, o_ref):
    o_ref[...] = (jnp.dot(ya_ref[...], wa_ref[...], preferred_element_type=F32)
                  + jnp.dot(yb_ref[...], wb_ref[...], preferred_element_type=F32))


def _outmm(ya, yb, w_out, tm=1024, tn=1024):
    t, aw = ya.shape
    d = w_out.shape[1]
    return pl.pallas_call(
        _outmm_kernel,
        grid=(t // tm, d // tn),
        in_specs=[pl.BlockSpec((tm, aw), lambda i, j: (i, 0)),
                  pl.BlockSpec((tm, aw), lambda i, j: (i, 0)),
                  pl.BlockSpec((aw, tn), lambda i, j: (0, j)),
                  pl.BlockSpec((aw, tn), lambda i, j: (1, j))],
        out_specs=pl.BlockSpec((tm, tn), lambda i, j: (i, j)),
        out_shape=jax.ShapeDtypeStruct((t, d), F32),
        compiler_params=_params("arbitrary", "arbitrary"),
        name="outmm",
    )(ya, yb, w_out, w_out)


def _postmix_kernel(y_ref, x_ref, gpost_ref, gate_ref, gpre_ref, sc_ref, sh_ref, wr_ref,
                    x1_ref, h2b_ref, h2p_ref, lg_ref, h2lo_ref, vec_ref):
    tm, d = x_ref.shape
    nr, pitch = _slab_rows(d), _slab_pitch(d)
    vec_ref[0:1, :] = gpost_ref[...] * gate_ref[0]
    vec_ref[1:2, :] = gpre_ref[...] * (1.0 + sc_ref[0])
    vec_ref[2:3, :] = sh_ref[0]

    def rows_of(r, u):
        return pl.multiple_of((r * NORM_UNROLL + u) * NORM_ROWS, NORM_ROWS)

    def body(r, carry):
        groups = [rows_of(r, u) for u in range(NORM_UNROLL)]
        invs = []
        for r0 in groups:
            rows = pl.ds(r0, NORM_ROWS)
            part = jnp.zeros((NORM_ROWS, V7X_LANES), F32)
            for c0 in range(0, d, NORM_COLS):
                y = y_ref[rows, c0:c0 + NORM_COLS]
                part = part + _fold_lanes(y * y)
            invs.append(_inv_rms(part, d))
        invs2 = []
        for r0, inv in zip(groups, invs):
            rows = pl.ds(r0, NORM_ROWS)
            part = jnp.zeros((NORM_ROWS, V7X_LANES), F32)
            for c0 in range(0, d, NORM_COLS):
                cols = slice(c0, c0 + NORM_COLS)
                x1 = x_ref[rows, cols] + (y_ref[rows, cols] * inv) * vec_ref[0:1, cols]
                x1_ref[rows, cols] = x1
                part = part + _fold_lanes(x1 * x1)
            invs2.append(_inv_rms(part, d))
        for r0, inv in zip(groups, invs2):
            rows = pl.ds(r0, NORM_ROWS)
            for c0 in range(0, d // 2, NORM_COLS):
                halves = []
                for cols in (slice(c0, c0 + NORM_COLS),
                             slice(d // 2 + c0, d // 2 + c0 + NORM_COLS)):
                    h2 = (x1_ref[rows, cols] * inv) * vec_ref[1:2, cols] + vec_ref[2:3, cols]
                    hb = h2.astype(BF16)
                    h2b_ref[rows, cols] = hb
                    hb32 = hb.astype(F32)
                    h2lo_ref[rows, cols] = (h2 - hb32).astype(BF16)
                    halves.append(hb32)
                words = _pack_pair(*halves)
                for q in range(NORM_COLS // V7X_LANES):
                    s = c0 // V7X_LANES + q
                    h2p_ref[pl.ds(r0 * pitch + s, NORM_ROWS, stride=pitch), :] = (
                        words[:, q * V7X_LANES:(q + 1) * V7X_LANES])
            for s in range(nr, pitch):
                h2p_ref[pl.ds(r0 * pitch + s, NORM_ROWS, stride=pitch), :] = (
                    jnp.zeros((NORM_ROWS, V7X_LANES), U32))
        return carry

    lax.fori_loop(0, tm // (NORM_ROWS * NORM_UNROLL), body, 0)
    w_hi, w_lo = _split_bf16(wr_ref[...], 2)
    p = jnp.dot(h2b_ref[...], jnp.concatenate([w_hi, w_lo], axis=1),
                preferred_element_type=F32)
    p_lo = jnp.dot(h2lo_ref[...], w_hi, preferred_element_type=F32)
    lg_ref[...] = p[:, :V7X_LANES] + p[:, V7X_LANES:] + p_lo


def _postmix(ymix, x2, gpost, mod3, gpre, w_rp, seq, tm=256):
    t, d = x2.shape
    per_b = seq // tm
    pitch = _slab_pitch(d)
    modspec = lambda m: pl.BlockSpec((1, 1, d), lambda i: ((i // per_b) * N_MOD + m, 0, 0))
    vecspec = pl.BlockSpec((1, d), lambda i: (0, 0))
    return pl.pallas_call(
        _postmix_kernel,
        grid=(t // tm,),
        in_specs=[pl.BlockSpec((tm, d), lambda i: (i, 0)),
                  pl.BlockSpec((tm, d), lambda i: (i, 0)),
                  vecspec, modspec(2), vecspec, modspec(4), modspec(3),
                  pl.BlockSpec((d, V7X_LANES), lambda i: (0, 0))],
        out_specs=[pl.BlockSpec((tm, d), lambda i: (i, 0)),
                   pl.BlockSpec((tm, d), lambda i: (i, 0)),
                   pl.BlockSpec((tm * pitch, V7X_LANES), lambda i: (i, 0)),
                   pl.BlockSpec((tm, V7X_LANES), lambda i: (i, 0))],
        out_shape=[jax.ShapeDtypeStruct((t, d), F32),
                   jax.ShapeDtypeStruct((t, d), BF16),
                   jax.ShapeDtypeStruct((t * pitch, V7X_LANES), U32),
                   jax.ShapeDtypeStruct((t, V7X_LANES), F32)],
        scratch_shapes=[pltpu.VMEM((tm, d), BF16),
                        pltpu.VMEM((8, d), F32)],
        compiler_params=_params("arbitrary"),
        name="postmix",
    )(ymix, x2, gpost, mod3, gpre, mod3, mod3, w_rp)


def _first_argmax(x, iota, axis, size):
    m = jnp.max(x, axis=axis, keepdims=True)
    idx = jnp.min(jnp.where(x == m, iota, size), axis=axis, keepdims=True)
    return m, idx


def _route_kernel(lg_ref, bias_ref, idx_ref, w_ref, rank_ref, cnt_ref, seen_ref):
    tt = lg_ref.shape[0]
    ne, ng = N_EXPERTS, N_GROUPS

    @pl.when(pl.program_id(0) == 0)
    def _():
        seen_ref[...] = jnp.zeros_like(seen_ref)

    gs = ne // ng
    neg = -jnp.inf
    scores = jax.nn.sigmoid(lg_ref[...].T[:ne, :])
    sel = scores + bias_ref[...]
    g3 = sel.reshape(ng, gs, tt)
    j_iota = lax.broadcasted_iota(jnp.int32, (ng, gs, tt), 1)
    m1, i1 = _first_argmax(g3, j_iota, 1, gs)
    m2 = jnp.max(jnp.where(j_iota == i1, neg, g3), axis=1, keepdims=True)
    grp = (m1 + m2).reshape(ng, tt)
    g_iota = lax.broadcasted_iota(jnp.int32, (ng, tt), 0)
    keep = jnp.zeros((ng, tt), jnp.bool_)
    for _ in range(TOPK_GROUPS):
        _, gi = _first_argmax(grp, g_iota, 0, ng)
        hit = g_iota == gi
        keep = jnp.logical_or(keep, hit)
        grp = jnp.where(hit, neg, grp)
    keep3 = jnp.broadcast_to(keep.reshape(ng, 1, tt), (ng, gs, tt))
    cand = jnp.where(keep3, g3, neg).reshape(ne, tt)
    e_iota = lax.broadcasted_iota(jnp.int32, (ne, tt), 0)
    idxs, ws, hits = [], [], []
    for _ in range(TOP_K):
        _, ei = _first_argmax(cand, e_iota, 0, ne)
        hit = e_iota == ei
        idxs.append(ei)
        hits.append(hit)
        ws.append(jnp.sum(jnp.where(hit, scores, 0.0), axis=0, keepdims=True))
        cand = jnp.where(hit, neg, cand)
    total = ws[0]
    for w in ws[1:]:
        total = total + w
    pad = idx_ref.shape[0] - TOP_K
    idx_ref[...] = jnp.concatenate(idxs + [jnp.zeros((pad, tt), jnp.int32)], axis=0)
    w_ref[...] = jnp.concatenate([w / total * ROUTED_SCALE for w in ws]
                                 + [jnp.zeros((pad, tt), F32)], axis=0)

    chosen = hits[0]
    for hit in hits[1:]:
        chosen = jnp.logical_or(chosen, hit)
    chosen = jnp.where(chosen, 1.0, 0.0)
    src = lax.broadcasted_iota(jnp.int32, (tt, tt), 0)
    dst = lax.broadcasted_iota(jnp.int32, (tt, tt), 1)
    before = jnp.where(src < dst, 1.0, 0.0).astype(BF16)
    rank_e = seen_ref[:, 0:1] + jnp.dot(chosen.astype(BF16), before,
                                        preferred_element_type=F32)
    ranks = [jnp.sum(jnp.where(hit, rank_e, 0.0), axis=0, keepdims=True) for hit in hits]
    rank_ref[...] = jnp.concatenate(ranks + [jnp.zeros((pad, tt), F32)],
                                    axis=0).astype(jnp.int32)
    seen_ref[...] = seen_ref[...] + jnp.sum(chosen, axis=1, keepdims=True)
    cnt_ref[...] = seen_ref[...].astype(jnp.int32)


def _route(logits, bias, tt=512):
    t = logits.shape[0]
    tok_spec = pl.BlockSpec((8, tt), lambda i: (0, i))
    return pl.pallas_call(
        _route_kernel,
        grid=(t // tt,),
        in_specs=[pl.BlockSpec((tt, V7X_LANES), lambda i: (i, 0)),
                  pl.BlockSpec((N_EXPERTS, 1), lambda i: (0, 0))],
        out_specs=[tok_spec, tok_spec, tok_spec,
                   pl.BlockSpec((N_EXPERTS, V7X_LANES), lambda i: (0, 0))],
        out_shape=[jax.ShapeDtypeStruct((8, t), jnp.int32),
                   jax.ShapeDtypeStruct((8, t), F32),
                   jax.ShapeDtypeStruct((8, t), jnp.int32),
                   jax.ShapeDtypeStruct((N_EXPERTS, V7X_LANES), jnp.int32)],
        scratch_shapes=[pltpu.VMEM((N_EXPERTS, V7X_LANES), F32)],
        compiler_params=_params("arbitrary"),
        name="route",
    )(logits, bias.reshape(N_EXPERTS, 1))


def _plan_kernel(idx_ref, rank_ref, cnt_ref, dest_ref, be_ref, nr_ref, tgt_ref, ws_ref,
                 *, bm, nch):
    ne = N_EXPERTS
    shift = bm.bit_length() - 1
    counts = cnt_ref[...]
    padded = ((counts + (bm - 1)) >> shift) << shift
    r = lax.broadcasted_iota(jnp.int32, (ne, ne), 0)
    c = lax.broadcasted_iota(jnp.int32, (ne, ne), 1)
    upto = jnp.where(c <= r, 1.0, 0.0)
    pend = jnp.dot(upto, padded.astype(F32), precision=HIGHEST,
                   preferred_element_type=F32).astype(jnp.int32)
    pstart = pend - padded
    idx = idx_ref[...]
    dest = rank_ref[...]
    blk = lax.broadcasted_iota(jnp.int32, be_ref.shape, 1)
    blk_row0 = blk * bm
    blk_e = jnp.zeros(be_ref.shape, jnp.int32)
    lane_e = lax.broadcasted_iota(jnp.int32, ws_ref.shape, 1)
    used_before = jnp.zeros(ws_ref.shape, jnp.int32)
    for e in range(ne):
        dest = dest + jnp.where(idx == e, pstart[e:e + 1, 0:1], 0)
        blk_e = blk_e + jnp.where(pend[e:e + 1, 0:1] <= blk_row0, 1, 0)
        used_before = used_before + jnp.where(
            lane_e > e, jnp.where(counts[e:e + 1, 0:1] > 0, 1, 0), 0)
    dest_ref[...] = dest
    blk_e = jnp.minimum(blk_e, ne - 1)
    be_ref[...] = blk_e
    nr_ref[...] = jnp.broadcast_to(pend[ne - 1:ne, :] >> shift, nr_ref.shape)
    ws_ref[...] = jnp.bitwise_and(used_before, 1)

    seg_end = jnp.zeros(be_ref.shape, jnp.int32)
    seg_start = jnp.zeros(be_ref.shape, jnp.int32)
    for e in range(ne):
        mine = blk_e == e
        seg_end = seg_end + jnp.where(mine, pend[e:e + 1, 0:1], 0)
        seg_start = seg_start + jnp.where(mine, pstart[e:e + 1, 0:1], 0)
    nxt = jnp.zeros(be_ref.shape, jnp.int32)
    for e in range(ne):
        nxt = nxt + jnp.where(pend[e:e + 1, 0:1] <= seg_end, 1, 0)
    need = jnp.where(nxt < ne, (nxt - blk_e) * nch, 0)
    n_blk = jnp.maximum((seg_end - seg_start) >> shift, 1)
    j = blk - (seg_start >> shift)
    per = (need.astype(F32) / n_blk.astype(F32)).astype(jnp.int32)
    per = per + jnp.where((per + 1) * n_blk <= need, 1, 0)
    per = per - jnp.where(per * n_blk > need, 1, 0)
    per = per + jnp.where(per * n_blk < need, 1, 0)
    tgt_ref[...] = (blk_e + 1) * nch + jnp.minimum(need, (j + 1) * per)


def _plan(idx8, rank8, cnt, bm, nb, nch, tt=2048):
    t = idx8.shape[1]
    tt = min(tt, t)
    nbp = -(-nb // V7X_LANES) * V7X_LANES
    tok_spec = pl.BlockSpec((8, tt), lambda i: (0, i))
    blk_spec = pl.BlockSpec((8, nbp), lambda i: (0, 0))
    one_spec = pl.BlockSpec((8, V7X_LANES), lambda i: (0, 0))
    return pl.pallas_call(
        functools.partial(_plan_kernel, bm=bm, nch=nch),
        grid=(t // tt,),
        in_specs=[tok_spec, tok_spec,
                  pl.BlockSpec((N_EXPERTS, V7X_LANES), lambda i: (0, 0))],
        out_specs=[tok_spec, blk_spec, one_spec, blk_spec, one_spec],
        out_shape=[jax.ShapeDtypeStruct((8, t), jnp.int32),
                   jax.ShapeDtypeStruct((8, nbp), jnp.int32),
                   jax.ShapeDtypeStruct((8, V7X_LANES), jnp.int32),
                   jax.ShapeDtypeStruct((8, nbp), jnp.int32),
                   jax.ShapeDtypeStruct((8, V7X_LANES), jnp.int32)],
        compiler_params=_params("arbitrary"),
        name="plan",
    )(idx8, rank8, cnt)


def _slots_kernel(cnt_ref, dest_ref, st_ref, *, bm, tt):
    i = pl.program_id(0)
    ns = st_ref.shape[0]

    @pl.when(i == 0)
    def _():
        def clear(lo, hi):
            def body(s, carry):
                st_ref[s] = 0
                return carry
            lax.fori_loop(lo, hi, body, 0)

        def per_expert(e, seg_start):
            n = cnt_ref[e]
            seg_end = seg_start + (n + (bm - 1)) // bm * bm
            clear(seg_start + n, seg_end)
            return seg_end

        used = lax.fori_loop(0, N_EXPERTS, per_expert, 0)
        clear(used, ns)

    def body(t, carry):
        for k in range(TOP_K):
            st_ref[dest_ref[k * tt + t]] = i * tt + t
        return carry

    lax.fori_loop(0, tt, body, 0, unroll=8)


def _slots(counts, dest8, bm, nb, tt=2048):
    rows, t = dest8.shape
    tt = min(tt, t)
    dest_chunks = dest8.reshape(rows, t // tt, tt).transpose(1, 0, 2).reshape(-1)
    return pl.pallas_call(
        functools.partial(_slots_kernel, bm=bm, tt=tt),
        grid=(t // tt,),
        in_specs=[pl.BlockSpec(memory_space=pltpu.SMEM),
                  pl.BlockSpec((rows * tt,), lambda i: (i,), memory_space=pltpu.SMEM)],
        out_specs=pl.BlockSpec(memory_space=pltpu.SMEM),
        out_shape=jax.ShapeDtypeStruct((nb * bm,), jnp.int32),
        compiler_params=_params("arbitrary"),
        name="slots",
    )(counts, dest_chunks)


def _row_gather(src_hbm, dst, sem, src_tok, dst_tok, nr, pitch):
    return pltpu.make_async_copy(src_hbm.at[pl.ds(src_tok * pitch, nr)],
                                 dst.at[pl.ds(dst_tok * pitch, nr)], sem)


def _gather_wait(src_hbm, dst, sem, n_tok, nr):
    pltpu.make_async_copy(src_hbm.at[pl.ds(0, n_tok * nr)],
                          dst.at[pl.ds(0, n_tok * nr)], sem).wait()


WCHUNK_ROWS = 512


def _wchunks(d):
    each = d // WCHUNK_ROWS
    return each, 3 * each


def _experts_kernel(be_ref, nr_ref, st_ref, tgt_ref, ws_ref,
                    h2s_hbm, wg_hbm, wu_hbm, wd_hbm, y_ref,
                    xbuf, sem, x_ref, wg_b, wu_b, wd_b, stg_a, stg_b, wsem, done_ref):
    i = pl.program_id(0)
    n_real = nr_ref[0]
    _, d, f = wg_b.shape
    nr, pitch = _slab_rows(d), _slab_pitch(d)
    bm = xbuf.shape[1] // pitch
    each, nch = _wchunks(d)
    dn_rows = f // each
    slot = lax.rem(i, 2)
    e_cur = be_ref[jnp.maximum(jnp.minimum(i, n_real - 1), 0)]
    limit = (be_ref[jnp.maximum(n_real - 1, 0)] + 1) * nch

    def start_gather(blk, sl):
        def body(r, carry):
            tok = st_ref[blk * bm + r]
            _row_gather(h2s_hbm, xbuf.at[sl], sem.at[sl], tok, r, nr, pitch).start()
            return carry
        lax.fori_loop(0, bm, body, 0, unroll=8)

    def chunk_parts(g):
        ex = g // nch
        c = g - ex * nch
        return ex, c

    def chunk_rows(kind, c):
        n = dn_rows if kind == 2 else WCHUNK_ROWS
        return pl.ds(lax.rem(c, each) * n, n)

    def chunk_copy(kind, ex, c, sl):
        src = (wg_hbm, wu_hbm, wd_hbm)[kind]
        stg = stg_b if kind == 2 else stg_a
        return pltpu.make_async_copy(src.at[ex, chunk_rows(kind, c), :], stg.at[sl],
                                     wsem.at[sl])

    def chunk_start(g, sl):
        ex, c = chunk_parts(g)
        for kind in range(3):
            @pl.when(c // each == kind)
            def _():
                chunk_copy(kind, ex, c, sl).start()

    def chunk_finish(g, sl):
        ex, c = chunk_parts(g)
        wsl = ws_ref[ex]
        for kind in range(3):
            @pl.when(c // each == kind)
            def _():
                chunk_copy(kind, ex, c, sl).wait()
                stg = stg_b if kind == 2 else stg_a
                dst = (wg_b, wu_b, wd_b)[kind]
                dst[wsl, chunk_rows(kind, c), :] = stg[sl].astype(BF16)

    def convert_until(target):
        def body(g, carry):
            sl = lax.rem(g, 2)
            chunk_finish(g, sl)

            @pl.when(g + 2 < limit)
            def _():
                chunk_start(g + 2, sl)
            return carry
        lax.fori_loop(done_ref[0], target, body, 0)
        done_ref[0] = jnp.maximum(done_ref[0], target)

    @pl.when(i == 0)
    def _():
        done_ref[0] = 0
        chunk_start(0, 0)
        chunk_start(1, 1)
        start_gather(0, 0)

    @pl.when(i + 1 < n_real)
    def _():
        start_gather(i + 1, 1 - slot)

    @pl.when(i < n_real)
    def _():
        convert_until((e_cur + 1) * nch)
        wsl = ws_ref[e_cur]
        _gather_wait(h2s_hbm, xbuf.at[slot], sem.at[slot], bm, nr)
        for kc in range(nr // SLAB_GROUP):
            lo, hi = _unpack_pair(_slab_cols(xbuf.at[slot], 0, bm, pitch, kc))
            x_ref[:, kc * SLAB_K:(kc + 1) * SLAB_K] = lo.astype(BF16)
            x_ref[:, d // 2 + kc * SLAB_K:d // 2 + (kc + 1) * SLAB_K] = hi.astype(BF16)
        x = x_ref[...]
        g = jnp.dot(x, wg_b[wsl], preferred_element_type=F32)
        u = jnp.dot(x, wu_b[wsl], preferred_element_type=F32)
        a = (_silu(g) * u).astype(BF16)
        for kc in range(nr // SLAB_GROUP):
            halves = [_round_bf16(jnp.dot(a, wd_b[wsl, :, c0:c0 + SLAB_K],
                                          preferred_element_type=F32))
                      for c0 in (kc * SLAB_K, d // 2 + kc * SLAB_K)]
            words = _pack_pair(*halves)
            for q in range(SLAB_GROUP):
                y_ref[pl.ds(kc * SLAB_GROUP + q, bm, stride=pitch), :] = (
                    words[:, q * V7X_LANES:(q + 1) * V7X_LANES])
        for s in range(nr, pitch):
            y_ref[pl.ds(s, bm, stride=pitch), :] = jnp.zeros((bm, V7X_LANES), U32)
        convert_until(tgt_ref[i])

    @pl.when(i >= n_real)
    def _():
        y_ref[...] = jnp.zeros_like(y_ref)


def _experts(block_e, n_real, slot_tok, tgt, wslot, h2s, wg, wu, wd, bm):
    nb = block_e.shape[0]
    ne, d, f = wg.shape
    pitch = _slab_pitch(d)
    each, _ = _wchunks(d)
    hbm = pl.BlockSpec(memory_space=pl.ANY)
    return pl.pallas_call(
        _experts_kernel,
        grid_spec=pltpu.PrefetchScalarGridSpec(
            num_scalar_prefetch=5,
            grid=(nb,),
            in_specs=[hbm, hbm, hbm, hbm],
            out_specs=pl.BlockSpec((bm * pitch, V7X_LANES), lambda i, *_: (i, 0)),
            scratch_shapes=[pltpu.VMEM((2, bm * pitch, V7X_LANES), U32),
                            pltpu.SemaphoreType.DMA((2,)),
                            pltpu.VMEM((bm, d), BF16),
                            pltpu.VMEM((2, d, f), BF16),
                            pltpu.VMEM((2, d, f), BF16),
                            pltpu.VMEM((2, f, d), BF16),
                            pltpu.VMEM((2, WCHUNK_ROWS, f), F32),
                            pltpu.VMEM((2, f // each, d), F32),
                            pltpu.SemaphoreType.DMA((2,)),
                            pltpu.SMEM((1,), jnp.int32)]),
        out_shape=jax.ShapeDtypeStruct((nb * bm * pitch, V7X_LANES), U32),
        compiler_params=_params("arbitrary"),
        name="experts",
    )(block_e, n_real, slot_tok, tgt, wslot, h2s, wg, wu, wd)


def _shared_kernel(h_ref, wg_ref, wu_ref, wd_ref, o_ref):
    h = h_ref[...]
    g = jnp.dot(h, wg_ref[...], preferred_element_type=F32)
    u = jnp.dot(h, wu_ref[...], preferred_element_type=F32)
    a = (_silu(g) * u).astype(BF16)
    o_ref[...] = jnp.dot(a, wd_ref[...], preferred_element_type=F32)


def _shared(h2b, wg, wu, wd, tm=256):
    t, d = h2b.shape
    f = wg.shape[1]
    return pl.pallas_call(
        _shared_kernel,
        grid=(t // tm,),
        in_specs=[pl.BlockSpec((tm, d), lambda i: (i, 0)),
                  pl.BlockSpec((d, f), lambda i: (0, 0)),
                  pl.BlockSpec((d, f), lambda i: (0, 0)),
                  pl.BlockSpec((f, d), lambda i: (0, 0))],
        out_specs=pl.BlockSpec((tm, d), lambda i: (i, 0)),
        out_shape=jax.ShapeDtypeStruct((t, d), F32),
        compiler_params=_params("arbitrary"),
        name="shared",
    )(h2b, wg, wu, wd)


def _combine_kernel(dest_ref, ys_hbm, w_ref, sh_ref, x1_ref, g_ref, gate_ref,
                    o_ref, gbuf, ybuf, vec_ref, sem):
    i = pl.program_id(0)
    n = pl.num_programs(0)
    tt, d = x1_ref.shape
    nr, pitch = _slab_rows(d), _slab_pitch(d)
    slot = lax.rem(i, 2)

    def start_gather(blk, sl):
        def body(r, carry):
            for k in range(TOP_K):
                row = dest_ref[k * (n * tt) + blk * tt + r]
                _row_gather(ys_hbm, gbuf.at[sl], sem.at[sl], row, k * tt + r,
                            nr, pitch).start()
            return carry
        lax.fori_loop(0, tt, body, 0, unroll=4)

    @pl.when(i == 0)
    def _():
        start_gather(0, 0)

    @pl.when(i + 1 < n)
    def _():
        start_gather(i + 1, 1 - slot)

    _gather_wait(ys_hbm, gbuf.at[slot], sem.at[slot], TOP_K * tt, nr)
    wk = [jnp.broadcast_to(w_ref[:, k:k + 1], (tt, V7X_LANES)) for k in range(TOP_K)]
    for s in range(nr):
        cols_lo = slice(s * V7X_LANES, (s + 1) * V7X_LANES)
        cols_hi = slice(d // 2 + s * V7X_LANES, d // 2 + (s + 1) * V7X_LANES)
        acc_lo = sh_ref[:, cols_lo]
        acc_hi = sh_ref[:, cols_hi]
        for k in range(TOP_K):
            lo, hi = _unpack_pair(gbuf[slot, pl.ds(k * tt * pitch + s, tt, stride=pitch), :])
            acc_lo = acc_lo + lo * wk[k]
            acc_hi = acc_hi + hi * wk[k]
        ybuf[:, cols_lo] = acc_lo
        ybuf[:, cols_hi] = acc_hi
    vec_ref[0:1, :] = g_ref[...] * gate_ref[0]

    def body(r, carry):
        groups = [pl.ds(pl.multiple_of((r * NORM_UNROLL + u) * NORM_ROWS, NORM_ROWS),
                        NORM_ROWS) for u in range(NORM_UNROLL)]
        invs = []
        for rows in groups:
            part = jnp.zeros((NORM_ROWS, V7X_LANES), F32)
            for c0 in range(0, d, NORM_COLS):
                yc = ybuf[rows, c0:c0 + NORM_COLS]
                part = part + _fold_lanes(yc * yc)
            invs.append(_inv_rms(part, d))
        for rows, inv in zip(groups, invs):
            for c0 in range(0, d, NORM_COLS):
                cols = slice(c0, c0 + NORM_COLS)
                o_ref[rows, cols] = (x1_ref[rows, cols]
                                     + (ybuf[rows, cols] * inv) * vec_ref[0:1, cols])
        return carry

    lax.fori_loop(0, tt // (NORM_ROWS * NORM_UNROLL), body, 0)


def _combine(dest, y_rows, wts, shared, x1, g, mod3, seq, tt=128):
    t, d = x1.shape
    pitch = _slab_pitch(d)
    per_b = seq // tt
    return pl.pallas_call(
        _combine_kernel,
        grid_spec=pltpu.PrefetchScalarGridSpec(
            num_scalar_prefetch=1,
            grid=(t // tt,),
            in_specs=[pl.BlockSpec(memory_space=pl.ANY),
                      pl.BlockSpec((tt, 8), lambda i, ds: (i, 0)),
                      pl.BlockSpec((tt, d), lambda i, ds: (i, 0)),
                      pl.BlockSpec((tt, d), lambda i, ds: (i, 0)),
                      pl.BlockSpec((1, d), lambda i, ds: (0, 0)),
                      pl.BlockSpec((1, 1, d), lambda i, ds: ((i // per_b) * N_MOD + 5, 0, 0))],
            out_specs=pl.BlockSpec((tt, d), lambda i, ds: (i, 0)),
            scratch_shapes=[pltpu.VMEM((2, TOP_K * tt * pitch, V7X_LANES), U32),
                            pltpu.VMEM((tt, d), F32),
                            pltpu.VMEM((8, d), F32),
                            pltpu.SemaphoreType.DMA((2,))]),
        out_shape=jax.ShapeDtypeStruct((t, d), F32),
        compiler_params=_params("arbitrary"),
        name="combine",
    )(dest, y_rows, wts, shared, x1, g, mod3)


EXPERT_BLOCK_ROWS = 128


def kernel(x, c, w_ada, b_ada, g_pre_mix, g_post_mix, g_pre_ffn, g_post_ffn, w_in,
           a_ln_g, a_ln_b, a_w_s, a_b_s, a_out_g, b_w_g2, b_b_g2, b_head_g, w_out,
           w_router, router_bias, we_gate, we_up, we_down, ws_gate, ws_up, ws_down):
    bsz, seq, d = x.shape
    t = bsz * seq
    aw = d // 2
    vw = d - aw
    kw = vw // 2
    n_main = 2 * aw + 2 * kw + 2 * vw
    x2 = x.reshape(t, d)
    for l in range(w_ada.shape[0]):
        mod = _ada(c, w_ada[l], b_ada[l])
        mod3 = mod.reshape(bsz * N_MOD, 1, d)

        w_gl = jnp.pad(w_in[l][:, n_main:], ((0, 0), (0, V7X_LANES - B_GATE_RANK))).astype(BF16)
        proj, g_low = _inproj(x2, g_pre_mix[l][None], mod3, w_in[l].astype(BF16), n_main,
                              w_gl, seq)
        y_a = _mixer_a(proj, a_ln_g[l][None], a_ln_b[l][None], a_w_s[l], a_b_s[l],
                       a_out_g[l][None], aw)
        w_g2p = jnp.pad(b_w_g2[l], ((0, V7X_LANES - B_GATE_RANK), (0, 0)))
        y_b = _mixer_b(proj, g_low, w_g2p, b_b_g2[l][None], b_head_g[l][None],
                       bsz, seq, kw, vw)

        w_rp = jnp.pad(w_router[l], ((0, 0), (0, V7X_LANES - N_EXPERTS)))
        ymix = _outmm(y_a, y_b, w_out[l].astype(BF16))
        x1, h2b, h2s, logits = _postmix(ymix, x2, g_post_mix[l][None], mod3,
                                        g_pre_ffn[l][None], w_rp, seq)
        idx8, w8, rank8, cnt = _route(logits, router_bias[l])
        bm = EXPERT_BLOCK_ROWS
        nb = -(-t * TOP_K // bm) + N_EXPERTS
        dest8, be8, nr8, tgt8, ws8 = _plan(idx8, rank8, cnt, bm, nb, _wchunks(d)[1])
        slot_tok = _slots(cnt[:, 0], dest8, bm, nb)
        y_rows = _experts(be8[0, :nb], nr8[0, :1], slot_tok, tgt8[0, :nb], ws8[0, :N_EXPERTS],
                          h2s, we_gate[l], we_up[l], we_down[l], bm)
        shared = _shared(h2b, ws_gate[l].astype(BF16), ws_up[l].astype(BF16),
                         ws_down[l].astype(BF16))
        x2 = _combine(dest8[:TOP_K].reshape(-1), y_rows, w8.T, shared, x1, g_post_ffn[l][None],
                      mod3, seq)
    return x2.reshape(bsz, seq, d)
```

```python
import functools

import jax
import jax.numpy as jnp
from jax import lax
from jax.experimental import pallas as pl
from jax.experimental.pallas import tpu as pltpu

F32 = jnp.float32
BF16 = jnp.bfloat16
HIGHEST = lax.Precision.HIGHEST

V7X_LANES = 128
V7X_SUBLANES = 8
V7X_VMEM_LIMIT_BYTES = 58 * 1024 * 1024

A_HEADS = 8
A_CHUNK = 128
B_HEADS = 4
B_GATE_RANK = 16
B_GATE_TAU = 16.0
B_CHUNK = 64
N_EXPERTS = 64
TOP_K = 6
N_GROUPS = 8
TOPK_GROUPS = 4
ROUTED_SCALE = 2.5
N_MOD = 6
EPS = 1e-6

SLAB_PAD = 4
U32 = jnp.uint32
HIGH_HALF = 0xFFFF0000


def _slab_rows(d):
    return d // (2 * V7X_LANES)


def _slab_pitch(d):
    return _slab_rows(d) + SLAB_PAD


def _round_bf16(x):
    return x.astype(BF16).astype(F32)


def _pack_pair(lo, hi):
    lo_bits = lax.bitcast_convert_type(lo, U32)
    hi_bits = lax.bitcast_convert_type(hi, U32)
    return jnp.bitwise_or(jnp.bitwise_and(hi_bits, U32(HIGH_HALF)),
                          jnp.right_shift(lo_bits, U32(16)))


def _unpack_pair(w):
    lo = lax.bitcast_convert_type(jnp.left_shift(w, U32(16)), F32)
    hi = lax.bitcast_convert_type(jnp.bitwise_and(w, U32(HIGH_HALF)), F32)
    return lo, hi


SLAB_GROUP = 4
SLAB_K = SLAB_GROUP * V7X_LANES


def _slab_cols(ref, first_tok, n_tok, pitch, kc):
    return jnp.concatenate(
        [ref[pl.ds(first_tok * pitch + kc * SLAB_GROUP + q, n_tok, stride=pitch), :]
         for q in range(SLAB_GROUP)], axis=-1)


def _params(*sem):
    return pltpu.CompilerParams(dimension_semantics=sem,
                                vmem_limit_bytes=V7X_VMEM_LIMIT_BYTES)


def _silu(x):
    return x * jax.nn.sigmoid(x)


ADA_ROWS = 64


def _ada_kernel(c_ref, w_ref, b_ref, o_ref, cb_ref):
    d, nb = c_ref.shape
    tn = w_ref.shape[1]

    @pl.when(pl.program_id(0) == 0)
    def _():
        def fill(i, carry):
            rows = pl.ds(pl.multiple_of(i * ADA_ROWS, ADA_ROWS), ADA_ROWS)
            cc = _silu(c_ref[rows, :])
            for b in range(nb):
                cb_ref[b, rows, :] = jnp.broadcast_to(cc[:, b:b + 1], (ADA_ROWS, V7X_LANES))
            return carry
        lax.fori_loop(0, d // ADA_ROWS, fill, 0)

    def body(i, accs):
        r0 = pl.multiple_of(i * ADA_ROWS, ADA_ROWS)
        w = w_ref[pl.ds(r0, ADA_ROWS), :]
        out = []
        for b in range(nb):
            cb = cb_ref[b, pl.ds(r0, ADA_ROWS), :]
            cols = []
            for q in range(tn // V7X_LANES):
                p = w[:, q * V7X_LANES:(q + 1) * V7X_LANES] * cb
                cols.append(p.reshape(ADA_ROWS // 8, 8, V7X_LANES).sum(axis=0))
            out.append(accs[b] + jnp.concatenate(cols, axis=-1))
        return tuple(out)

    accs = lax.fori_loop(0, d // ADA_ROWS, body,
                         tuple(jnp.zeros((8, tn), F32) for _ in range(nb)))
    for b in range(nb):
        o_ref[b:b + 1, :] = accs[b].sum(axis=0, keepdims=True) + b_ref[...]


def _ada(c, w_ada, b_ada, tn=1024):
    nb, d = c.shape
    n = w_ada.shape[1]
    return pl.pallas_call(
        _ada_kernel,
        grid=(n // tn,),
        in_specs=[pl.BlockSpec((d, nb), lambda j: (0, 0)),
                  pl.BlockSpec((d, tn), lambda j: (0, j)),
                  pl.BlockSpec((1, tn), lambda j: (0, j))],
        out_specs=pl.BlockSpec((nb, tn), lambda j: (0, j)),
        out_shape=jax.ShapeDtypeStruct((nb, n), F32),
        scratch_shapes=[pltpu.VMEM((nb, d, V7X_LANES), F32)],
        compiler_params=_params("arbitrary"),
        name="ada",
    )(c.T, w_ada, b_ada.reshape(1, n))


NORM_ROWS = 16
NORM_UNROLL = 2


def _rms_rows(x, g):
    ms = jnp.mean(x * x, axis=-1, keepdims=True)
    return x * lax.rsqrt(ms + EPS) * g


NORM_COLS = 512


def _inproj_kernel(x_ref, g_ref, sc_ref, sh_ref, w_ref, wgl_ref, o_ref, gl_ref, h_ref,
                   vec_ref):
    tm, d = x_ref.shape

    @pl.when(pl.program_id(1) == 0)
    def _():
        vec_ref[0:1, :] = g_ref[...] * (1.0 + sc_ref[0])
        vec_ref[1:2, :] = sh_ref[0]

        def body(r, carry):
            groups = [pl.ds(pl.multiple_of((r * NORM_UNROLL + u) * NORM_ROWS, NORM_ROWS),
                            NORM_ROWS) for u in range(NORM_UNROLL)]
            invs = []
            for rows in groups:
                part = jnp.zeros((NORM_ROWS, V7X_LANES), F32)
                for c0 in range(0, d, NORM_COLS):
                    xc = x_ref[rows, c0:c0 + NORM_COLS]
                    part = part + _fold_lanes(xc * xc)
                invs.append(_inv_rms(part, d))
            for rows, inv in zip(groups, invs):
                for c0 in range(0, d, NORM_COLS):
                    cols = slice(c0, c0 + NORM_COLS)
                    h = (x_ref[rows, cols] * inv) * vec_ref[0:1, cols] + vec_ref[1:2, cols]
                    h_ref[rows, cols] = h.astype(BF16)
            return carry

        lax.fori_loop(0, tm // (NORM_ROWS * NORM_UNROLL), body, 0)
        gl_ref[...] = jnp.dot(h_ref[...], wgl_ref[...], preferred_element_type=F32)

    o_ref[...] = jnp.dot(h_ref[...], w_ref[...],
                         preferred_element_type=F32).astype(o_ref.dtype)


def _inproj(x2, g, mod3, w_all, n, w_gl, seq, tm=512, tn=1024):
    t, d = x2.shape
    per_b = seq // tm
    return pl.pallas_call(
        _inproj_kernel,
        grid=(t // tm, n // tn),
        in_specs=[pl.BlockSpec((tm, d), lambda i, j: (i, 0)),
                  pl.BlockSpec((1, d), lambda i, j: (0, 0)),
                  pl.BlockSpec((1, 1, d), lambda i, j: ((i // per_b) * N_MOD + 1, 0, 0)),
                  pl.BlockSpec((1, 1, d), lambda i, j: ((i // per_b) * N_MOD + 0, 0, 0)),
                  pl.BlockSpec((d, tn), lambda i, j: (0, j)),
                  pl.BlockSpec((d, V7X_LANES), lambda i, j: (0, 0))],
        out_specs=[pl.BlockSpec((tm, tn), lambda i, j: (i, j)),
                   pl.BlockSpec((tm, V7X_LANES), lambda i, j: (i, 0))],
        out_shape=[jax.ShapeDtypeStruct((t, n), BF16),
                   jax.ShapeDtypeStruct((t, V7X_LANES), F32)],
        scratch_shapes=[pltpu.VMEM((tm, d), BF16), pltpu.VMEM((8, d), F32)],
        compiler_params=_params("arbitrary", "arbitrary"),
        name="inproj",
    )(x2, g, mod3, mod3, w_all, w_gl)


def _mixer_a_kernel(u_ref, v_ref, lng_ref, lnb_ref, ws_ref, bs_ref, og_ref, o_ref, y_ref):
    tc, aw = u_ref.shape
    nh, c, _ = ws_ref.shape
    hd = aw // nh
    row = lax.broadcasted_iota(jnp.int32, (c, c), 0)
    col = lax.broadcasted_iota(jnp.int32, (c, c), 1)
    causal = col <= row
    for ci in range(tc // c):
        rows = slice(ci * c, (ci + 1) * c)
        v = v_ref[rows, :].astype(F32)
        mu = jnp.mean(v, axis=-1, keepdims=True)
        vc = v - mu
        var = jnp.mean(vc * vc, axis=-1, keepdims=True)
        vn = (vc * lax.rsqrt(var + EPS) * lng_ref[...] + lnb_ref[...]).astype(BF16)
        ssq = jnp.zeros((c, 1), F32)
        for h in range(nh):
            cols = slice(h * hd, (h + 1) * hd)
            w = jnp.where(causal, ws_ref[h], 0.0).astype(BF16)
            s = jnp.dot(w, vn[:, cols], preferred_element_type=F32) + bs_ref[:, h:h + 1]
            y = u_ref[rows, cols].astype(F32) * s
            y_ref[:, cols] = y
            ssq = ssq + jnp.sum(y * y, axis=-1, keepdims=True)
        inv = lax.rsqrt(ssq / aw + EPS)
        o_ref[rows, :] = (y_ref[...] * inv * og_ref[...]).astype(o_ref.dtype)


def _mixer_a(proj, ln_g, ln_b, w_s, b_s, out_g, aw, tc=256):
    t = proj.shape[0]
    nh, c, _ = w_s.shape
    return pl.pallas_call(
        _mixer_a_kernel,
        grid=(t // tc,),
        in_specs=[pl.BlockSpec((tc, aw), lambda i: (i, 0)),
                  pl.BlockSpec((tc, aw), lambda i: (i, 1)),
                  pl.BlockSpec((1, aw), lambda i: (0, 0)),
                  pl.BlockSpec((1, aw), lambda i: (0, 0)),
                  pl.BlockSpec((nh, c, c), lambda i: (0, 0, 0)),
                  pl.BlockSpec((c, nh), lambda i: (0, 0)),
                  pl.BlockSpec((1, aw), lambda i: (0, 0))],
        out_specs=pl.BlockSpec((tc, aw), lambda i: (i, 0)),
        out_shape=jax.ShapeDtypeStruct((t, aw), BF16),
        scratch_shapes=[pltpu.VMEM((c, aw), F32)],
        compiler_params=_params("arbitrary"),
        name="mixer_a",
    )(proj, proj, ln_g, ln_b, w_s, b_s.T, out_g)


def _split_bf16(x, terms):
    out = []
    for _ in range(terms - 1):
        hi = x.astype(BF16)
        out.append(hi)
        x = x - hi.astype(F32)
    out.append(x.astype(BF16))
    return out


_NT = (((1,), (1,)), ((), ()))
_TN = (((0,), (0,)), ((), ()))


def _mixer_b_kernel(q_ref, k_ref, v_ref, r_ref, gl_ref, wg2_ref, bg2_ref, hg_ref,
                    o_ref, st_ref, cum_ref):
    tb, kw = q_ref.shape
    vw = v_ref.shape[1]
    nh = st_ref.shape[0]
    dk, dv = kw // nh, vw // nh
    c = B_CHUNK
    dot = functools.partial(jnp.dot, preferred_element_type=F32)

    @pl.when(pl.program_id(1) == 0)
    def _():
        st_ref[...] = jnp.zeros_like(st_ref)

    gl_hi, gl_lo = _split_bf16(gl_ref[...], 2)
    w_hi, w_lo = _split_bf16(wg2_ref[...], 2)
    logits = dot(gl_hi, w_hi) + dot(gl_hi, w_lo) + dot(gl_lo, w_hi) + bg2_ref[...]
    log_a = jax.nn.log_sigmoid(logits) / B_GATE_TAU
    row = lax.broadcasted_iota(jnp.int32, (tb, tb), 0)
    col = lax.broadcasted_iota(jnp.int32, (tb, tb), 1)
    chunk_start = row - jnp.bitwise_and(row, c - 1)
    tri = jnp.where(col <= row, jnp.where(col >= chunk_start, 1.0, 0.0), 0.0).astype(BF16)
    cum = None
    for part in _split_bf16(log_a, 3):
        cum = dot(tri, part) if cum is None else cum + dot(tri, part)
    cum_ref[...] = cum

    crow = lax.broadcasted_iota(jnp.int32, (c, c), 0)
    ccol = lax.broadcasted_iota(jnp.int32, (c, c), 1)
    causal = ccol <= crow
    for ci in range(tb // c):
        rows = slice(ci * c, (ci + 1) * c)
        for h in range(nh):
            ks = slice(h * dk, (h + 1) * dk)
            vs = slice(h * dv, (h + 1) * dv)
            cum_h = cum_ref[rows, ks]
            last = cum_h[c - 1:c, :]
            q = q_ref[rows, ks].astype(F32) * (dk ** -0.5)
            k = k_ref[rows, ks].astype(F32)
            q_dec = (q * jnp.exp(cum_h)).astype(BF16)
            k_dec = (k * jnp.exp(-cum_h)).astype(BF16)
            k_state = (k * jnp.exp(last - cum_h)).astype(BF16)
            vh = v_ref[rows, vs]
            attn = lax.dot_general(q_dec, k_dec, _NT, preferred_element_type=F32)
            attn = jnp.where(causal, attn, 0.0).astype(BF16)
            state_t = st_ref[h]
            o = dot(attn, vh) + lax.dot_general(q_dec, state_t.astype(BF16), _NT,
                                                preferred_element_type=F32)
            kv_t = lax.dot_general(vh, k_state, _TN, preferred_element_type=F32)
            st_ref[h] = state_t * jnp.exp(last) + kv_t
            o = _rms_rows(o, hg_ref[...])
            o_ref[rows, vs] = (o * _silu(r_ref[rows, vs].astype(F32))).astype(o_ref.dtype)


def _mixer_b(proj, g_low, w_g2p, b_g2, head_g, bsz, seq, kw, vw, tb=256):
    t = proj.shape[0]
    per_b = seq // tb
    q_blk = (2 * vw) // kw
    v_blk = (2 * vw + 2 * kw) // vw
    row = lambda b, n: b * per_b + n
    return pl.pallas_call(
        _mixer_b_kernel,
        grid=(bsz, per_b),
        in_specs=[pl.BlockSpec((tb, kw), lambda b, n: (row(b, n), q_blk)),
                  pl.BlockSpec((tb, kw), lambda b, n: (row(b, n), q_blk + 1)),
                  pl.BlockSpec((tb, vw), lambda b, n: (row(b, n), v_blk)),
                  pl.BlockSpec((tb, vw), lambda b, n: (row(b, n), v_blk + 1)),
                  pl.BlockSpec((tb, V7X_LANES), lambda b, n: (row(b, n), 0)),
                  pl.BlockSpec((V7X_LANES, kw), lambda b, n: (0, 0)),
                  pl.BlockSpec((1, kw), lambda b, n: (0, 0)),
                  pl.BlockSpec((1, vw // B_HEADS), lambda b, n: (0, 0))],
        out_specs=pl.BlockSpec((tb, vw), lambda b, n: (row(b, n), 0)),
        out_shape=jax.ShapeDtypeStruct((t, vw), BF16),
        scratch_shapes=[pltpu.VMEM((B_HEADS, vw // B_HEADS, kw // B_HEADS), F32),
                        pltpu.VMEM((tb, kw), F32)],
        compiler_params=_params("arbitrary", "arbitrary"),
        name="mixer_b",
    )(proj, proj, proj, proj, g_low, w_g2p, b_g2, head_g)


def _fold_lanes(v):
    out = v[:, :V7X_LANES]
    for q in range(1, v.shape[1] // V7X_LANES):
        out = out + v[:, q * V7X_LANES:(q + 1) * V7X_LANES]
    return out


def _inv_rms(part, d):
    return lax.rsqrt(jnp.sum(part, axis=-1, keepdims=True) / d + EPS)


def _outmm_kernel(ya_ref, yb_ref, wa_ref, wb_ref, o_ref):
    o_ref[...] = (jnp.dot(ya_ref[...], wa_ref[...], preferred_element_type=F32)
                  + jnp.dot(yb_ref[...], wb_ref[...], preferred_element_type=F32))


def _outmm(ya, yb, w_out, tm=1024, tn=1024):
    t, aw = ya.shape
    d = w_out.shape[1]
    return pl.pallas_call(
        _outmm_kernel,
        grid=(t // tm, d // tn),
        in_specs=[pl.BlockSpec((tm, aw), lambda i, j: (i, 0)),
                  pl.BlockSpec((tm, aw), lambda i, j: (i, 0)),
                  pl.BlockSpec((aw, tn), lambda i, j: (0, j)),
                  pl.BlockSpec((aw, tn), lambda i, j: (1, j))],
        out_specs=pl.BlockSpec((tm, tn), lambda i, j: (i, j)),
        out_shape=jax.ShapeDtypeStruct((t, d), F32),
        compiler_params=_params("arbitrary", "arbitrary"),
        name="outmm",
    )(ya, yb, w_out, w_out)


def _postmix_kernel(y_ref, x_ref, gpost_ref, gate_ref, gpre_ref, sc_ref, sh_ref, wr_ref,
                    x1_ref, h2b_ref, h2p_ref, lg_ref, h2lo_ref, vec_ref):
    tm, d = x_ref.shape
    nr, pitch = _slab_rows(d), _slab_pitch(d)
    vec_ref[0:1, :] = gpost_ref[...] * gate_ref[0]
    vec_ref[1:2, :] = gpre_ref[...] * (1.0 + sc_ref[0])
    vec_ref[2:3, :] = sh_ref[0]

    def rows_of(r, u):
        return pl.multiple_of((r * NORM_UNROLL + u) * NORM_ROWS, NORM_ROWS)

    def body(r, carry):
        groups = [rows_of(r, u) for u in range(NORM_UNROLL)]
        invs = []
        for r0 in groups:
            rows = pl.ds(r0, NORM_ROWS)
            part = jnp.zeros((NORM_ROWS, V7X_LANES), F32)
            for c0 in range(0, d, NORM_COLS):
                y = y_ref[rows, c0:c0 + NORM_COLS]
                part = part + _fold_lanes(y * y)
            invs.append(_inv_rms(part, d))
        invs2 = []
        for r0, inv in zip(groups, invs):
            rows = pl.ds(r0, NORM_ROWS)
            part = jnp.zeros((NORM_ROWS, V7X_LANES), F32)
            for c0 in range(0, d, NORM_COLS):
                cols = slice(c0, c0 + NORM_COLS)
                x1 = x_ref[rows, cols] + (y_ref[rows, cols] * inv) * vec_ref[0:1, cols]
                x1_ref[rows, cols] = x1
                part = part + _fold_lanes(x1 * x1)
            invs2.append(_inv_rms(part, d))
        for r0, inv in zip(groups, invs2):
            rows = pl.ds(r0, NORM_ROWS)
            for c0 in range(0, d // 2, NORM_COLS):
                halves = []
                for cols in (slice(c0, c0 + NORM_COLS),
                             slice(d // 2 + c0, d // 2 + c0 + NORM_COLS)):
                    h2 = (x1_ref[rows, cols] * inv) * vec_ref[1:2, cols] + vec_ref[2:3, cols]
                    hb = h2.astype(BF16)
                    h2b_ref[rows, cols] = hb
                    hb32 = hb.astype(F32)
                    h2lo_ref[rows, cols] = (h2 - hb32).astype(BF16)
                    halves.append(hb32)
                words = _pack_pair(*halves)
                for q in range(NORM_COLS // V7X_LANES):
                    s = c0 // V7X_LANES + q
                    h2p_ref[pl.ds(r0 * pitch + s, NORM_ROWS, stride=pitch), :] = (
                        words[:, q * V7X_LANES:(q + 1) * V7X_LANES])
            for s in range(nr, pitch):
                h2p_ref[pl.ds(r0 * pitch + s, NORM_ROWS, stride=pitch), :] = (
                    jnp.zeros((NORM_ROWS, V7X_LANES), U32))
        return carry

    lax.fori_loop(0, tm // (NORM_ROWS * NORM_UNROLL), body, 0)
    w_hi, w_lo = _split_bf16(wr_ref[...], 2)
    p = jnp.dot(h2b_ref[...], jnp.concatenate([w_hi, w_lo], axis=1),
                preferred_element_type=F32)
    p_lo = jnp.dot(h2lo_ref[...], w_hi, preferred_element_type=F32)
    lg_ref[...] = p[:, :V7X_LANES] + p[:, V7X_LANES:] + p_lo


def _postmix(ymix, x2, gpost, mod3, gpre, w_rp, seq, tm=256):
    t, d = x2.shape
    per_b = seq // tm
    pitch = _slab_pitch(d)
    modspec = lambda m: pl.BlockSpec((1, 1, d), lambda i: ((i // per_b) * N_MOD + m, 0, 0))
    vecspec = pl.BlockSpec((1, d), lambda i: (0, 0))
    return pl.pallas_call(
        _postmix_kernel,
        grid=(t // tm,),
        in_specs=[pl.BlockSpec((tm, d), lambda i: (i, 0)),
                  pl.BlockSpec((tm, d), lambda i: (i, 0)),
                  vecspec, modspec(2), vecspec, modspec(4), modspec(3),
                  pl.BlockSpec((d, V7X_LANES), lambda i: (0, 0))],
        out_specs=[pl.BlockSpec((tm, d), lambda i: (i, 0)),
                   pl.BlockSpec((tm, d), lambda i: (i, 0)),
                   pl.BlockSpec((tm * pitch, V7X_LANES), lambda i: (i, 0)),
                   pl.BlockSpec((tm, V7X_LANES), lambda i: (i, 0))],
        out_shape=[jax.ShapeDtypeStruct((t, d), F32),
                   jax.ShapeDtypeStruct((t, d), BF16),
                   jax.ShapeDtypeStruct((t * pitch, V7X_LANES), U32),
                   jax.ShapeDtypeStruct((t, V7X_LANES), F32)],
        scratch_shapes=[pltpu.VMEM((tm, d), BF16),
                        pltpu.VMEM((8, d), F32)],
        compiler_params=_params("arbitrary"),
        name="postmix",
    )(ymix, x2, gpost, mod3, gpre, mod3, mod3, w_rp)


def _first_argmax(x, iota, axis, size):
    m = jnp.max(x, axis=axis, keepdims=True)
    idx = jnp.min(jnp.where(x == m, iota, size), axis=axis, keepdims=True)
    return m, idx


def _route_kernel(lg_ref, bias_ref, idx_ref, w_ref, rank_ref, cnt_ref, seen_ref):
    tt = lg_ref.shape[0]
    ne, ng = N_EXPERTS, N_GROUPS

    @pl.when(pl.program_id(0) == 0)
    def _():
        seen_ref[...] = jnp.zeros_like(seen_ref)

    gs = ne // ng
    neg = -jnp.inf
    scores = jax.nn.sigmoid(lg_ref[...].T[:ne, :])
    sel = scores + bias_ref[...]
    g3 = sel.reshape(ng, gs, tt)
    j_iota = lax.broadcasted_iota(jnp.int32, (ng, gs, tt), 1)
    m1, i1 = _first_argmax(g3, j_iota, 1, gs)
    m2 = jnp.max(jnp.where(j_iota == i1, neg, g3), axis=1, keepdims=True)
    grp = (m1 + m2).reshape(ng, tt)
    g_iota = lax.broadcasted_iota(jnp.int32, (ng, tt), 0)
    keep = jnp.zeros((ng, tt), jnp.bool_)
    for _ in range(TOPK_GROUPS):
        _, gi = _first_argmax(grp, g_iota, 0, ng)
        hit = g_iota == gi
        keep = jnp.logical_or(keep, hit)
        grp = jnp.where(hit, neg, grp)
    keep3 = jnp.broadcast_to(keep.reshape(ng, 1, tt), (ng, gs, tt))
    cand = jnp.where(keep3, g3, neg).reshape(ne, tt)
    e_iota = lax.broadcasted_iota(jnp.int32, (ne, tt), 0)
    idxs, ws, hits = [], [], []
    for _ in range(TOP_K):
        _, ei = _first_argmax(cand, e_iota, 0, ne)
        hit = e_iota == ei
        idxs.append(ei)
        hits.append(hit)
        ws.append(jnp.sum(jnp.where(hit, scores, 0.0), axis=0, keepdims=True))
        cand = jnp.where(hit, neg, cand)
    total = ws[0]
    for w in ws[1:]:
        total = total + w
    pad = idx_ref.shape[0] - TOP_K
    idx_ref[...] = jnp.concatenate(idxs + [jnp.zeros((pad, tt), jnp.int32)], axis=0)
    w_ref[...] = jnp.concatenate([w / total * ROUTED_SCALE for w in ws]
                                 + [jnp.zeros((pad, tt), F32)], axis=0)

    chosen = hits[0]
    for hit in hits[1:]:
        chosen = jnp.logical_or(chosen, hit)
    chosen = jnp.where(chosen, 1.0, 0.0)
    src = lax.broadcasted_iota(jnp.int32, (tt, tt), 0)
    dst = lax.broadcasted_iota(jnp.int32, (tt, tt), 1)
    before = jnp.where(src < dst, 1.0, 0.0).astype(BF16)
    rank_e = seen_ref[:, 0:1] + jnp.dot(chosen.astype(BF16), before,
                                        preferred_element_type=F32)
    ranks = [jnp.sum(jnp.where(hit, rank_e, 0.0), axis=0, keepdims=True) for hit in hits]
    rank_ref[...] = jnp.concatenate(ranks + [jnp.zeros((pad, tt), F32)],
                                    axis=0).astype(jnp.int32)
    seen_ref[...] = seen_ref[...] + jnp.sum(chosen, axis=1, keepdims=True)
    cnt_ref[...] = seen_ref[...].astype(jnp.int32)


def _route(logits, bias, tt=512):
    t = logits.shape[0]
    tok_spec = pl.BlockSpec((8, tt), lambda i: (0, i))
    return pl.pallas_call(
        _route_kernel,
        grid=(t // tt,),
        in_specs=[pl.BlockSpec((tt, V7X_LANES), lambda i: (i, 0)),
                  pl.BlockSpec((N_EXPERTS, 1), lambda i: (0, 0))],
        out_specs=[tok_spec, tok_spec, tok_spec,
                   pl.BlockSpec((N_EXPERTS, V7X_LANES), lambda i: (0, 0))],
        out_shape=[jax.ShapeDtypeStruct((8, t), jnp.int32),
                   jax.ShapeDtypeStruct((8, t), F32),
                   jax.ShapeDtypeStruct((8, t), jnp.int32),
                   jax.ShapeDtypeStruct((N_EXPERTS, V7X_LANES), jnp.int32)],
        scratch_shapes=[pltpu.VMEM((N_EXPERTS, V7X_LANES), F32)],
        compiler_params=_params("arbitrary"),
        name="route",
    )(logits, bias.reshape(N_EXPERTS, 1))


def _plan_kernel(idx_ref, rank_ref, cnt_ref, dest_ref, be_ref, nr_ref, tgt_ref, ws_ref,
                 *, bm, nch):
    ne = N_EXPERTS
    shift = bm.bit_length() - 1
    counts = cnt_ref[...]
    padded = ((counts + (bm - 1)) >> shift) << shift
    r = lax.broadcasted_iota(jnp.int32, (ne, ne), 0)
    c = lax.broadcasted_iota(jnp.int32, (ne, ne), 1)
    upto = jnp.where(c <= r, 1.0, 0.0)
    pend = jnp.dot(upto, padded.astype(F32), precision=HIGHEST,
                   preferred_element_type=F32).astype(jnp.int32)
    pstart = pend - padded
    idx = idx_ref[...]
    dest = rank_ref[...]
    blk = lax.broadcasted_iota(jnp.int32, be_ref.shape, 1)
    blk_row0 = blk * bm
    blk_e = jnp.zeros(be_ref.shape, jnp.int32)
    lane_e = lax.broadcasted_iota(jnp.int32, ws_ref.shape, 1)
    used_before = jnp.zeros(ws_ref.shape, jnp.int32)
    for e in range(ne):
        dest = dest + jnp.where(idx == e, pstart[e:e + 1, 0:1], 0)
        blk_e = blk_e + jnp.where(pend[e:e + 1, 0:1] <= blk_row0, 1, 0)
        used_before = used_before + jnp.where(
            lane_e > e, jnp.where(counts[e:e + 1, 0:1] > 0, 1, 0), 0)
    dest_ref[...] = dest
    blk_e = jnp.minimum(blk_e, ne - 1)
    be_ref[...] = blk_e
    nr_ref[...] = jnp.broadcast_to(pend[ne - 1:ne, :] >> shift, nr_ref.shape)
    ws_ref[...] = jnp.bitwise_and(used_before, 1)

    seg_end = jnp.zeros(be_ref.shape, jnp.int32)
    seg_start = jnp.zeros(be_ref.shape, jnp.int32)
    for e in range(ne):
        mine = blk_e == e
        seg_end = seg_end + jnp.where(mine, pend[e:e + 1, 0:1], 0)
        seg_start = seg_start + jnp.where(mine, pstart[e:e + 1, 0:1], 0)
    nxt = jnp.zeros(be_ref.shape, jnp.int32)
    for e in range(ne):
        nxt = nxt + jnp.where(pend[e:e + 1, 0:1] <= seg_end, 1, 0)
    need = jnp.where(nxt < ne, (nxt - blk_e) * nch, 0)
    n_blk = jnp.maximum((seg_end - seg_start) >> shift, 1)
    j = blk - (seg_start >> shift)
    per = (need.astype(F32) / n_blk.astype(F32)).astype(jnp.int32)
    per = per + jnp.where((per + 1) * n_blk <= need, 1, 0)
    per = per - jnp.where(per * n_blk > need, 1, 0)
    per = per + jnp.where(per * n_blk < need, 1, 0)
    tgt_ref[...] = (blk_e + 1) * nch + jnp.minimum(need, (j + 1) * per)


def _plan(idx8, rank8, cnt, bm, nb, nch, tt=2048):
    t = idx8.shape[1]
    tt = min(tt, t)
    nbp = -(-nb // V7X_LANES) * V7X_LANES
    tok_spec = pl.BlockSpec((8, tt), lambda i: (0, i))
    blk_spec = pl.BlockSpec((8, nbp), lambda i: (0, 0))
    one_spec = pl.BlockSpec((8, V7X_LANES), lambda i: (0, 0))
    return pl.pallas_call(
        functools.partial(_plan_kernel, bm=bm, nch=nch),
        grid=(t // tt,),
        in_specs=[tok_spec, tok_spec,
                  pl.BlockSpec((N_EXPERTS, V7X_LANES), lambda i: (0, 0))],
        out_specs=[tok_spec, blk_spec, one_spec, blk_spec, one_spec],
        out_shape=[jax.ShapeDtypeStruct((8, t), jnp.int32),
                   jax.ShapeDtypeStruct((8, nbp), jnp.int32),
                   jax.ShapeDtypeStruct((8, V7X_LANES), jnp.int32),
                   jax.ShapeDtypeStruct((8, nbp), jnp.int32),
                   jax.ShapeDtypeStruct((8, V7X_LANES), jnp.int32)],
        compiler_params=_params("arbitrary"),
        name="plan",
    )(idx8, rank8, cnt)


def _row_gather(src_hbm, dst, sem, src_tok, dst_tok, nr, pitch):
    return pltpu.make_async_copy(src_hbm.at[pl.ds(src_tok * pitch, nr)],
                                 dst.at[pl.ds(dst_tok * pitch, nr)], sem)


def _gather_wait(src_hbm, dst, sem, n_tok, nr):
    pltpu.make_async_copy(src_hbm.at[pl.ds(0, n_tok * nr)],
                          dst.at[pl.ds(0, n_tok * nr)], sem).wait()


WCHUNK_ROWS = 512


def _wchunks(d):
    each = d // WCHUNK_ROWS
    return each, 3 * each


def _experts_kernel(be_ref, nr_ref, st_ref, tgt_ref, ws_ref,
                    h2s_hbm, wg_hbm, wu_hbm, wd_hbm, y_ref,
                    xbuf0, xbuf1, sem, x_ref, wg_b, wu_b, wd_b, stg_a, stg_b, wsem, done_ref):
    xbufs = (xbuf0, xbuf1)
    i = pl.program_id(0)
    n_real = nr_ref[0]
    _, d, f = wg_b.shape
    nr, pitch = _slab_rows(d), _slab_pitch(d)
    bm = xbuf0.shape[0] // pitch
    each, nch = _wchunks(d)
    dn_rows = f // each
    slot = lax.rem(i, 2)
    e_cur = be_ref[jnp.maximum(jnp.minimum(i, n_real - 1), 0)]
    limit = (be_ref[jnp.maximum(n_real - 1, 0)] + 1) * nch

    def chunk_parts(g):
        ex = g // nch
        c = g - ex * nch
        return ex, c

    def chunk_rows(kind, c):
        n = dn_rows if kind == 2 else WCHUNK_ROWS
        return pl.ds(lax.rem(c, each) * n, n)

    def chunk_copy(kind, ex, c, sl):
        src = (wg_hbm, wu_hbm, wd_hbm)[kind]
        stg = stg_b if kind == 2 else stg_a
        return pltpu.make_async_copy(src.at[ex, chunk_rows(kind, c), :], stg.at[sl],
                                     wsem.at[sl])

    def chunk_start(g, sl):
        ex, c = chunk_parts(g)
        for kind in range(3):
            @pl.when(c // each == kind)
            def _():
                chunk_copy(kind, ex, c, sl).start()

    def chunk_finish(g, sl):
        ex, c = chunk_parts(g)
        wsl = ws_ref[ex]
        for kind in range(3):
            @pl.when(c // each == kind)
            def _():
                chunk_copy(kind, ex, c, sl).wait()
                stg = stg_b if kind == 2 else stg_a
                dst = (wg_b, wu_b, wd_b)[kind]
                dst[wsl, chunk_rows(kind, c), :] = stg[sl].astype(BF16)

    def convert_until(target):
        def body(g, carry):
            sl = lax.rem(g, 2)
            chunk_finish(g, sl)

            @pl.when(g + 2 < limit)
            def _():
                chunk_start(g + 2, sl)
            return carry
        lax.fori_loop(done_ref[0], target, body, 0)
        done_ref[0] = jnp.maximum(done_ref[0], target)

    @pl.when(i == 0)
    def _():
        done_ref[0] = 0
        chunk_start(0, 0)
        chunk_start(1, 1)

        def body(r, carry):
            _row_gather(h2s_hbm, xbufs[0], sem.at[0], st_ref[r], r, nr, pitch).start()
            return carry
        lax.fori_loop(0, bm, body, 0, unroll=8)

    def run_block(cur):
        nxt = 1 - cur
        convert_until((e_cur + 1) * nch)
        wsl = ws_ref[e_cur]
        _gather_wait(h2s_hbm, xbufs[cur], sem.at[cur], bm, nr)
        for kc in range(nr // SLAB_GROUP):
            lo, hi = _unpack_pair(_slab_cols(xbufs[cur], 0, bm, pitch, kc))
            x_ref[:, kc * SLAB_K:(kc + 1) * SLAB_K] = lo.astype(BF16)
            x_ref[:, d // 2 + kc * SLAB_K:d // 2 + (kc + 1) * SLAB_K] = hi.astype(BF16)
        nxt_blk = jnp.minimum(i + 1, n_real - 1)
        for r in range(bm):
            _row_gather(h2s_hbm, xbufs[nxt], sem.at[nxt], st_ref[nxt_blk * bm + r], r,
                        nr, pitch).start()
        x = x_ref[...]
        g = jnp.dot(x, wg_b[wsl], preferred_element_type=F32)
        u = jnp.dot(x, wu_b[wsl], preferred_element_type=F32)
        a = (_silu(g) * u).astype(BF16)
        for kc in range(nr // SLAB_GROUP):
            halves = [_round_bf16(jnp.dot(a, wd_b[wsl, :, c0:c0 + SLAB_K],
                                          preferred_element_type=F32))
                      for c0 in (kc * SLAB_K, d // 2 + kc * SLAB_K)]
            words = _pack_pair(*halves)
            for q in range(SLAB_GROUP):
                y_ref[pl.ds(kc * SLAB_GROUP + q, bm, stride=pitch), :] = (
                    words[:, q * V7X_LANES:(q + 1) * V7X_LANES])
        for s in range(nr, pitch):
            y_ref[pl.ds(s, bm, stride=pitch), :] = jnp.zeros((bm, V7X_LANES), U32)
        convert_until(tgt_ref[i])

        @pl.when(i + 1 == n_real)
        def _():
            _gather_wait(h2s_hbm, xbufs[nxt], sem.at[nxt], bm, nr)

    for cur in range(2):
        @pl.when(jnp.logical_and(i < n_real, slot == cur))
        def _():
            run_block(cur)

    @pl.when(i >= n_real)
    def _():
        y_ref[...] = jnp.zeros_like(y_ref)


def _experts(block_e, n_real, slot_tok, tgt, wslot, h2s, wg, wu, wd, bm):
    nb = block_e.shape[0]
    ne, d, f = wg.shape
    pitch = _slab_pitch(d)
    each, _ = _wchunks(d)
    hbm = pl.BlockSpec(memory_space=pl.ANY)
    return pl.pallas_call(
        _experts_kernel,
        grid_spec=pltpu.PrefetchScalarGridSpec(
            num_scalar_prefetch=5,
            grid=(nb,),
            in_specs=[hbm, hbm, hbm, hbm],
            out_specs=pl.BlockSpec((bm * pitch, V7X_LANES), lambda i, *_: (i, 0)),
            scratch_shapes=[pltpu.VMEM((bm * pitch, V7X_LANES), U32),
                            pltpu.VMEM((bm * pitch, V7X_LANES), U32),
                            pltpu.SemaphoreType.DMA((2,)),
                            pltpu.VMEM((bm, d), BF16),
                            pltpu.VMEM((2, d, f), BF16),
                            pltpu.VMEM((2, d, f), BF16),
                            pltpu.VMEM((2, f, d), BF16),
                            pltpu.VMEM((2, WCHUNK_ROWS, f), F32),
                            pltpu.VMEM((2, f // each, d), F32),
                            pltpu.SemaphoreType.DMA((2,)),
                            pltpu.SMEM((1,), jnp.int32)]),
        out_shape=jax.ShapeDtypeStruct((nb * bm * pitch, V7X_LANES), U32),
        compiler_params=_params("arbitrary"),
        name="experts",
    )(block_e, n_real, slot_tok, tgt, wslot, h2s, wg, wu, wd)


def _shared_kernel(cnt_ref, dest_ref, h_ref, wg_ref, wu_ref, wd_ref, o_ref, st_ref, *, bm):
    i = pl.program_id(0)
    tm = h_ref.shape[0]
    ns = st_ref.shape[0]

    @pl.when(i == 0)
    def _():
        def clear(lo, hi):
            def body(s, carry):
                st_ref[s] = 0
                return carry
            lax.fori_loop(lo, hi, body, 0)

        def per_expert(e, seg_start):
            n = cnt_ref[e]
            seg_end = seg_start + (n + (bm - 1)) // bm * bm
            clear(seg_start + n, seg_end)
            return seg_end

        used = lax.fori_loop(0, N_EXPERTS, per_expert, 0)
        clear(used, ns)

    for t in range(tm):
        for k in range(TOP_K):
            st_ref[dest_ref[k * tm + t]] = i * tm + t

    h = h_ref[...]
    g = jnp.dot(h, wg_ref[...], preferred_element_type=F32)
    u = jnp.dot(h, wu_ref[...], preferred_element_type=F32)
    a = (_silu(g) * u).astype(BF16)
    o_ref[...] = jnp.dot(a, wd_ref[...], preferred_element_type=F32)


def _shared(counts, dest8, h2b, wg, wu, wd, bm, nb, tm=256):
    t, d = h2b.shape
    f = wg.shape[1]
    rows = dest8.shape[0]
    dest_tiles = dest8.reshape(rows, t // tm, tm).transpose(1, 0, 2).reshape(-1)
    return pl.pallas_call(
        functools.partial(_shared_kernel, bm=bm),
        grid=(t // tm,),
        in_specs=[pl.BlockSpec(memory_space=pltpu.SMEM),
                  pl.BlockSpec((rows * tm,), lambda i: (i,), memory_space=pltpu.SMEM),
                  pl.BlockSpec((tm, d), lambda i: (i, 0)),
                  pl.BlockSpec((d, f), lambda i: (0, 0)),
                  pl.BlockSpec((d, f), lambda i: (0, 0)),
                  pl.BlockSpec((f, d), lambda i: (0, 0))],
        out_specs=[pl.BlockSpec((tm, d), lambda i: (i, 0)),
                   pl.BlockSpec(memory_space=pltpu.SMEM)],
        out_shape=[jax.ShapeDtypeStruct((t, d), F32),
                   jax.ShapeDtypeStruct((nb * bm,), jnp.int32)],
        compiler_params=_params("arbitrary"),
        name="shared",
    )(counts, dest_tiles, h2b, wg, wu, wd)


def _combine_kernel(dest_ref, ys_hbm, w_ref, sh_ref, x1_ref, g_ref, gate_ref,
                    o_ref, gbuf0, gbuf1, ybuf, vec_ref, sem):
    gbufs = (gbuf0, gbuf1)
    i = pl.program_id(0)
    n = pl.num_programs(0)
    tt, d = x1_ref.shape
    nr, pitch = _slab_rows(d), _slab_pitch(d)
    slot = lax.rem(i, 2)

    def gather_row(blk, buf, r, k):
        row = dest_ref[k * (n * tt) + blk * tt + r]
        _row_gather(ys_hbm, gbufs[buf], sem.at[buf], row, k * tt + r, nr, pitch).start()

    @pl.when(i == 0)
    def _():
        def body(r, carry):
            for k in range(TOP_K):
                gather_row(0, 0, r, k)
            return carry
        lax.fori_loop(0, tt, body, 0, unroll=4)

    def weighted_sum(cur):
        nxt = 1 - cur
        _gather_wait(ys_hbm, gbufs[cur], sem.at[cur], TOP_K * tt, nr)
        nxt_blk = jnp.minimum(i + 1, n - 1)
        for r in range(tt):
            for k in range(TOP_K):
                gather_row(nxt_blk, nxt, r, k)
        wk = [jnp.broadcast_to(w_ref[:, k:k + 1], (tt, V7X_LANES)) for k in range(TOP_K)]
        for s in range(nr):
            cols_lo = slice(s * V7X_LANES, (s + 1) * V7X_LANES)
            cols_hi = slice(d // 2 + s * V7X_LANES, d // 2 + (s + 1) * V7X_LANES)
            acc_lo = sh_ref[:, cols_lo]
            acc_hi = sh_ref[:, cols_hi]
            for k in range(TOP_K):
                lo, hi = _unpack_pair(
                    gbufs[cur][pl.ds(k * tt * pitch + s, tt, stride=pitch), :])
                acc_lo = acc_lo + lo * wk[k]
                acc_hi = acc_hi + hi * wk[k]
            ybuf[:, cols_lo] = acc_lo
            ybuf[:, cols_hi] = acc_hi

        @pl.when(i + 1 == n)
        def _():
            _gather_wait(ys_hbm, gbufs[nxt], sem.at[nxt], TOP_K * tt, nr)

    for cur in range(2):
        @pl.when(slot == cur)
        def _():
            weighted_sum(cur)

    vec_ref[0:1, :] = g_ref[...] * gate_ref[0]

    def body(r, carry):
        groups = [pl.ds(pl.multiple_of((r * NORM_UNROLL + u) * NORM_ROWS, NORM_ROWS),
                        NORM_ROWS) for u in range(NORM_UNROLL)]
        invs = []
        for rows in groups:
            part = jnp.zeros((NORM_ROWS, V7X_LANES), F32)
            for c0 in range(0, d, NORM_COLS):
                yc = ybuf[rows, c0:c0 + NORM_COLS]
                part = part + _fold_lanes(yc * yc)
            invs.append(_inv_rms(part, d))
        for rows, inv in zip(groups, invs):
            for c0 in range(0, d, NORM_COLS):
                cols = slice(c0, c0 + NORM_COLS)
                o_ref[rows, cols] = (x1_ref[rows, cols]
                                     + (ybuf[rows, cols] * inv) * vec_ref[0:1, cols])
        return carry

    lax.fori_loop(0, tt // (NORM_ROWS * NORM_UNROLL), body, 0)


def _combine(dest, y_rows, wts, shared, x1, g, mod3, seq, tt=128):
    t, d = x1.shape
    pitch = _slab_pitch(d)
    per_b = seq // tt
    return pl.pallas_call(
        _combine_kernel,
        grid_spec=pltpu.PrefetchScalarGridSpec(
            num_scalar_prefetch=1,
            grid=(t // tt,),
            in_specs=[pl.BlockSpec(memory_space=pl.ANY),
                      pl.BlockSpec((tt, 8), lambda i, ds: (i, 0)),
                      pl.BlockSpec((tt, d), lambda i, ds: (i, 0)),
                      pl.BlockSpec((tt, d), lambda i, ds: (i, 0)),
                      pl.BlockSpec((1, d), lambda i, ds: (0, 0)),
                      pl.BlockSpec((1, 1, d), lambda i, ds: ((i // per_b) * N_MOD + 5, 0, 0))],
            out_specs=pl.BlockSpec((tt, d), lambda i, ds: (i, 0)),
            scratch_shapes=[pltpu.VMEM((TOP_K * tt * pitch, V7X_LANES), U32),
                            pltpu.VMEM((TOP_K * tt * pitch, V7X_LANES), U32),
                            pltpu.VMEM((tt, d), F32),
                            pltpu.VMEM((8, d), F32),
                            pltpu.SemaphoreType.DMA((2,))]),
        out_shape=jax.ShapeDtypeStruct((t, d), F32),
        compiler_params=_params("arbitrary"),
        name="combine",
    )(dest, y_rows, wts, shared, x1, g, mod3)


EXPERT_BLOCK_ROWS = 128


def kernel(x, c, w_ada, b_ada, g_pre_mix, g_post_mix, g_pre_ffn, g_post_ffn, w_in,
           a_ln_g, a_ln_b, a_w_s, a_b_s, a_out_g, b_w_g2, b_b_g2, b_head_g, w_out,
           w_router, router_bias, we_gate, we_up, we_down, ws_gate, ws_up, ws_down):
    bsz, seq, d = x.shape
    t = bsz * seq
    aw = d // 2
    vw = d - aw
    kw = vw // 2
    n_main = 2 * aw + 2 * kw + 2 * vw
    x2 = x.reshape(t, d)
    for l in range(w_ada.shape[0]):
        mod = _ada(c, w_ada[l], b_ada[l])
        mod3 = mod.reshape(bsz * N_MOD, 1, d)

        w_gl = jnp.pad(w_in[l][:, n_main:], ((0, 0), (0, V7X_LANES - B_GATE_RANK))).astype(BF16)
        proj, g_low = _inproj(x2, g_pre_mix[l][None], mod3, w_in[l].astype(BF16), n_main,
                              w_gl, seq)
        y_a = _mixer_a(proj, a_ln_g[l][None], a_ln_b[l][None], a_w_s[l], a_b_s[l],
                       a_out_g[l][None], aw)
        w_g2p = jnp.pad(b_w_g2[l], ((0, V7X_LANES - B_GATE_RANK), (0, 0)))
        y_b = _mixer_b(proj, g_low, w_g2p, b_b_g2[l][None], b_head_g[l][None],
                       bsz, seq, kw, vw)

        w_rp = jnp.pad(w_router[l], ((0, 0), (0, V7X_LANES - N_EXPERTS)))
        ymix = _outmm(y_a, y_b, w_out[l].astype(BF16))
        x1, h2b, h2s, logits = _postmix(ymix, x2, g_post_mix[l][None], mod3,
                                        g_pre_ffn[l][None], w_rp, seq)
        idx8, w8, rank8, cnt = _route(logits, router_bias[l])
        bm = EXPERT_BLOCK_ROWS
        nb = -(-t * TOP_K // bm) + N_EXPERTS
        dest8, be8, nr8, tgt8, ws8 = _plan(idx8, rank8, cnt, bm, nb, _wchunks(d)[1])
        shared, slot_tok = _shared(cnt[:, 0], dest8, h2b, ws_gate[l].astype(BF16),
                                   ws_up[l].astype(BF16), ws_down[l].astype(BF16), bm, nb)
        y_rows = _experts(be8[0, :nb], nr8[0, :1], slot_tok, tgt8[0, :nb], ws8[0, :N_EXPERTS],
                          h2s, we_gate[l], we_up[l], we_down[l], bm)
        x2 = _combine(dest8[:TOP_K].reshape(-1), y_rows, w8.T, shared, x1, g_post_ffn[l][None],
                      mod3, seq)
    return x2.reshape(bsz, seq, d)
```

```python
import functools

import jax
import jax.numpy as jnp
from jax import lax
from jax.experimental import pallas as pl
from jax.experimental.pallas import tpu as pltpu

F32 = jnp.float32
BF16 = jnp.bfloat16
HIGHEST = lax.Precision.HIGHEST

V7X_LANES = 128
V7X_SUBLANES = 8
V7X_VMEM_LIMIT_BYTES = 58 * 1024 * 1024

A_HEADS = 8
A_CHUNK = 128
B_HEADS = 4
B_GATE_RANK = 16
B_GATE_TAU = 16.0
B_CHUNK = 64
N_EXPERTS = 64
TOP_K = 6
N_GROUPS = 8
TOPK_GROUPS = 4
ROUTED_SCALE = 2.5
N_MOD = 6
EPS = 1e-6

SLAB_PAD = 4
U32 = jnp.uint32
HIGH_HALF = 0xFFFF0000


def _slab_rows(d):
    return d // (2 * V7X_LANES)


def _slab_pitch(d):
    return _slab_rows(d) + SLAB_PAD


def _round_bf16(x):
    return x.astype(BF16).astype(F32)


def _pack_pair(lo, hi):
    lo_bits = lax.bitcast_convert_type(lo, U32)
    hi_bits = lax.bitcast_convert_type(hi, U32)
    return jnp.bitwise_or(jnp.bitwise_and(hi_bits, U32(HIGH_HALF)),
                          jnp.right_shift(lo_bits, U32(16)))


def _unpack_pair(w):
    lo = lax.bitcast_convert_type(jnp.left_shift(w, U32(16)), F32)
    hi = lax.bitcast_convert_type(jnp.bitwise_and(w, U32(HIGH_HALF)), F32)
    return lo, hi


SLAB_GROUP = 4
SLAB_K = SLAB_GROUP * V7X_LANES


def _slab_cols(ref, first_tok, n_tok, pitch, kc):
    return jnp.concatenate(
        [ref[pl.ds(first_tok * pitch + kc * SLAB_GROUP + q, n_tok, stride=pitch), :]
         for q in range(SLAB_GROUP)], axis=-1)


def _params(*sem):
    return pltpu.CompilerParams(dimension_semantics=sem,
                                vmem_limit_bytes=V7X_VMEM_LIMIT_BYTES)


def _silu(x):
    return x * jax.nn.sigmoid(x)


ADA_ROWS = 64


def _ada_kernel(c_ref, w_ref, b_ref, o_ref, cb_ref):
    d, nb = c_ref.shape
    tn = w_ref.shape[1]

    @pl.when(pl.program_id(0) == 0)
    def _():
        def fill(i, carry):
            rows = pl.ds(pl.multiple_of(i * ADA_ROWS, ADA_ROWS), ADA_ROWS)
            cc = _silu(c_ref[rows, :])
            for b in range(nb):
                cb_ref[b, rows, :] = jnp.broadcast_to(cc[:, b:b + 1], (ADA_ROWS, V7X_LANES))
            return carry
        lax.fori_loop(0, d // ADA_ROWS, fill, 0)

    def body(i, accs):
        r0 = pl.multiple_of(i * ADA_ROWS, ADA_ROWS)
        w = w_ref[pl.ds(r0, ADA_ROWS), :]
        out = []
        for b in range(nb):
            cb = cb_ref[b, pl.ds(r0, ADA_ROWS), :]
            cols = []
            for q in range(tn // V7X_LANES):
                p = w[:, q * V7X_LANES:(q + 1) * V7X_LANES] * cb
                cols.append(p.reshape(ADA_ROWS // 8, 8, V7X_LANES).sum(axis=0))
            out.append(accs[b] + jnp.concatenate(cols, axis=-1))
        return tuple(out)

    accs = lax.fori_loop(0, d // ADA_ROWS, body,
                         tuple(jnp.zeros((8, tn), F32) for _ in range(nb)))
    for b in range(nb):
        o_ref[b:b + 1, :] = accs[b].sum(axis=0, keepdims=True) + b_ref[...]


def _ada(c, w_ada, b_ada, tn=1024):
    nb, d = c.shape
    n = w_ada.shape[1]
    return pl.pallas_call(
        _ada_kernel,
        grid=(n // tn,),
        in_specs=[pl.BlockSpec((d, nb), lambda j: (0, 0)),
                  pl.BlockSpec((d, tn), lambda j: (0, j)),
                  pl.BlockSpec((1, tn), lambda j: (0, j))],
        out_specs=pl.BlockSpec((nb, tn), lambda j: (0, j)),
        out_shape=jax.ShapeDtypeStruct((nb, n), F32),
        scratch_shapes=[pltpu.VMEM((nb, d, V7X_LANES), F32)],
        compiler_params=_params("arbitrary"),
        name="ada",
    )(c.T, w_ada, b_ada.reshape(1, n))


NORM_ROWS = 16
NORM_UNROLL = 2


def _rms_rows(x, g):
    ms = jnp.mean(x * x, axis=-1, keepdims=True)
    return x * lax.rsqrt(ms + EPS) * g


NORM_COLS = 512


def _inproj_kernel(x_ref, g_ref, sc_ref, sh_ref, w_ref, wgl_ref, o_ref, gl_ref, h_ref,
                   vec_ref):
    tm, d = x_ref.shape

    @pl.when(pl.program_id(1) == 0)
    def _():
        vec_ref[0:1, :] = g_ref[...] * (1.0 + sc_ref[0])
        vec_ref[1:2, :] = sh_ref[0]

        def body(r, carry):
            groups = [pl.ds(pl.multiple_of((r * NORM_UNROLL + u) * NORM_ROWS, NORM_ROWS),
                            NORM_ROWS) for u in range(NORM_UNROLL)]
            invs = []
            for rows in groups:
                part = jnp.zeros((NORM_ROWS, V7X_LANES), F32)
                for c0 in range(0, d, NORM_COLS):
                    xc = x_ref[rows, c0:c0 + NORM_COLS]
                    part = part + _fold_lanes(xc * xc)
                invs.append(_inv_rms(part, d))
            for rows, inv in zip(groups, invs):
                for c0 in range(0, d, NORM_COLS):
                    cols = slice(c0, c0 + NORM_COLS)
                    h = (x_ref[rows, cols] * inv) * vec_ref[0:1, cols] + vec_ref[1:2, cols]
                    h_ref[rows, cols] = h.astype(BF16)
            return carry

        lax.fori_loop(0, tm // (NORM_ROWS * NORM_UNROLL), body, 0)
        gl_ref[...] = jnp.dot(h_ref[...], wgl_ref[...], preferred_element_type=F32)

    o_ref[...] = jnp.dot(h_ref[...], w_ref[...],
                         preferred_element_type=F32).astype(o_ref.dtype)


def _inproj(x2, g, mod3, w_all, n, w_gl, seq, tm=512, tn=1024):
    t, d = x2.shape
    per_b = seq // tm
    return pl.pallas_call(
        _inproj_kernel,
        grid=(t // tm, n // tn),
        in_specs=[pl.BlockSpec((tm, d), lambda i, j: (i, 0)),
                  pl.BlockSpec((1, d), lambda i, j: (0, 0)),
                  pl.BlockSpec((1, 1, d), lambda i, j: ((i // per_b) * N_MOD + 1, 0, 0)),
                  pl.BlockSpec((1, 1, d), lambda i, j: ((i // per_b) * N_MOD + 0, 0, 0)),
                  pl.BlockSpec((d, tn), lambda i, j: (0, j)),
                  pl.BlockSpec((d, V7X_LANES), lambda i, j: (0, 0))],
        out_specs=[pl.BlockSpec((tm, tn), lambda i, j: (i, j)),
                   pl.BlockSpec((tm, V7X_LANES), lambda i, j: (i, 0))],
        out_shape=[jax.ShapeDtypeStruct((t, n), BF16),
                   jax.ShapeDtypeStruct((t, V7X_LANES), F32)],
        scratch_shapes=[pltpu.VMEM((tm, d), BF16), pltpu.VMEM((8, d), F32)],
        compiler_params=_params("arbitrary", "arbitrary"),
        name="inproj",
    )(x2, g, mod3, mod3, w_all, w_gl)


def _mixer_a_kernel(u_ref, v_ref, lng_ref, lnb_ref, ws_ref, bs_ref, og_ref, o_ref, y_ref):
    tc, aw = u_ref.shape
    nh, c, _ = ws_ref.shape
    hd = aw // nh
    row = lax.broadcasted_iota(jnp.int32, (c, c), 0)
    col = lax.broadcasted_iota(jnp.int32, (c, c), 1)
    causal = col <= row
    for ci in range(tc // c):
        rows = slice(ci * c, (ci + 1) * c)
        v = v_ref[rows, :].astype(F32)
        mu = jnp.mean(v, axis=-1, keepdims=True)
        vc = v - mu
        var = jnp.mean(vc * vc, axis=-1, keepdims=True)
        vn = (vc * lax.rsqrt(var + EPS) * lng_ref[...] + lnb_ref[...]).astype(BF16)
        ssq = jnp.zeros((c, 1), F32)
        for h in range(nh):
            cols = slice(h * hd, (h + 1) * hd)
            w = jnp.where(causal, ws_ref[h], 0.0).astype(BF16)
            s = jnp.dot(w, vn[:, cols], preferred_element_type=F32) + bs_ref[:, h:h + 1]
            y = u_ref[rows, cols].astype(F32) * s
            y_ref[:, cols] = y
            ssq = ssq + jnp.sum(y * y, axis=-1, keepdims=True)
        inv = lax.rsqrt(ssq / aw + EPS)
        o_ref[rows, :] = (y_ref[...] * inv * og_ref[...]).astype(o_ref.dtype)


def _mixer_a(proj, ln_g, ln_b, w_s, b_s, out_g, aw, tc=256):
    t = proj.shape[0]
    nh, c, _ = w_s.shape
    return pl.pallas_call(
        _mixer_a_kernel,
        grid=(t // tc,),
        in_specs=[pl.BlockSpec((tc, aw), lambda i: (i, 0)),
                  pl.BlockSpec((tc, aw), lambda i: (i, 1)),
                  pl.BlockSpec((1, aw), lambda i: (0, 0)),
                  pl.BlockSpec((1, aw), lambda i: (0, 0)),
                  pl.BlockSpec((nh, c, c), lambda i: (0, 0, 0)),
                  pl.BlockSpec((c, nh), lambda i: (0, 0)),
                  pl.BlockSpec((1, aw), lambda i: (0, 0))],
        out_specs=pl.BlockSpec((tc, aw), lambda i: (i, 0)),
        out_shape=jax.ShapeDtypeStruct((t, aw), BF16),
        scratch_shapes=[pltpu.VMEM((c, aw), F32)],
        compiler_params=_params("arbitrary"),
        name="mixer_a",
    )(proj, proj, ln_g, ln_b, w_s, b_s.T, out_g)


def _split_bf16(x, terms):
    out = []
    for _ in range(terms - 1):
        hi = x.astype(BF16)
        out.append(hi)
        x = x - hi.astype(F32)
    out.append(x.astype(BF16))
    return out


_NT = (((1,), (1,)), ((), ()))
_TN = (((0,), (0,)), ((), ()))


def _mixer_b_kernel(q_ref, k_ref, v_ref, r_ref, gl_ref, wg2_ref, bg2_ref, hg_ref,
                    o_ref, st_ref, cum_ref):
    tb, kw = q_ref.shape
    vw = v_ref.shape[1]
    nh = st_ref.shape[0]
    dk, dv = kw // nh, vw // nh
    c = B_CHUNK
    dot = functools.partial(jnp.dot, preferred_element_type=F32)

    @pl.when(pl.program_id(1) == 0)
    def _():
        st_ref[...] = jnp.zeros_like(st_ref)

    gl_hi, gl_lo = _split_bf16(gl_ref[...], 2)
    w_hi, w_lo = _split_bf16(wg2_ref[...], 2)
    logits = dot(gl_hi, w_hi) + dot(gl_hi, w_lo) + dot(gl_lo, w_hi) + bg2_ref[...]
    log_a = jax.nn.log_sigmoid(logits) / B_GATE_TAU
    row = lax.broadcasted_iota(jnp.int32, (tb, tb), 0)
    col = lax.broadcasted_iota(jnp.int32, (tb, tb), 1)
    chunk_start = row - jnp.bitwise_and(row, c - 1)
    tri = jnp.where(col <= row, jnp.where(col >= chunk_start, 1.0, 0.0), 0.0).astype(BF16)
    cum = None
    for part in _split_bf16(log_a, 3):
        cum = dot(tri, part) if cum is None else cum + dot(tri, part)
    cum_ref[...] = cum

    crow = lax.broadcasted_iota(jnp.int32, (c, c), 0)
    ccol = lax.broadcasted_iota(jnp.int32, (c, c), 1)
    causal = ccol <= crow
    for ci in range(tb // c):
        rows = slice(ci * c, (ci + 1) * c)
        for h in range(nh):
            ks = slice(h * dk, (h + 1) * dk)
            vs = slice(h * dv, (h + 1) * dv)
            cum_h = cum_ref[rows, ks]
            last = cum_h[c - 1:c, :]
            q = q_ref[rows, ks].astype(F32) * (dk ** -0.5)
            k = k_ref[rows, ks].astype(F32)
            q_dec = (q * jnp.exp(cum_h)).astype(BF16)
            k_dec = (k * jnp.exp(-cum_h)).astype(BF16)
            k_state = (k * jnp.exp(last - cum_h)).astype(BF16)
            vh = v_ref[rows, vs]
            attn = lax.dot_general(q_dec, k_dec, _NT, preferred_element_type=F32)
            attn = jnp.where(causal, attn, 0.0).astype(BF16)
            state_t = st_ref[h]
            o = dot(attn, vh) + lax.dot_general(q_dec, state_t.astype(BF16), _NT,
                                                preferred_element_type=F32)
            kv_t = lax.dot_general(vh, k_state, _TN, preferred_element_type=F32)
            st_ref[h] = state_t * jnp.exp(last) + kv_t
            o = _rms_rows(o, hg_ref[...])
            o_ref[rows, vs] = (o * _silu(r_ref[rows, vs].astype(F32))).astype(o_ref.dtype)


def _mixer_b(proj, g_low, w_g2p, b_g2, head_g, bsz, seq, kw, vw, tb=256):
    t = proj.shape[0]
    per_b = seq // tb
    q_blk = (2 * vw) // kw
    v_blk = (2 * vw + 2 * kw) // vw
    row = lambda b, n: b * per_b + n
    return pl.pallas_call(
        _mixer_b_kernel,
        grid=(bsz, per_b),
        in_specs=[pl.BlockSpec((tb, kw), lambda b, n: (row(b, n), q_blk)),
                  pl.BlockSpec((tb, kw), lambda b, n: (row(b, n), q_blk + 1)),
                  pl.BlockSpec((tb, vw), lambda b, n: (row(b, n), v_blk)),
                  pl.BlockSpec((tb, vw), lambda b, n: (row(b, n), v_blk + 1)),
                  pl.BlockSpec((tb, V7X_LANES), lambda b, n: (row(b, n), 0)),
                  pl.BlockSpec((V7X_LANES, kw), lambda b, n: (0, 0)),
                  pl.BlockSpec((1, kw), lambda b, n: (0, 0)),
                  pl.BlockSpec((1, vw // B_HEADS), lambda b, n: (0, 0))],
        out_specs=pl.BlockSpec((tb, vw), lambda b, n: (row(b, n), 0)),
        out_shape=jax.ShapeDtypeStruct((t, vw), BF16),
        scratch_shapes=[pltpu.VMEM((B_HEADS, vw // B_HEADS, kw // B_HEADS), F32),
                        pltpu.VMEM((tb, kw), F32)],
        compiler_params=_params("arbitrary", "arbitrary"),
        name="mixer_b",
    )(proj, proj, proj, proj, g_low, w_g2p, b_g2, head_g)


def _fold_lanes(v):
    out = v[:, :V7X_LANES]
    for q in range(1, v.shape[1] // V7X_LANES):
        out = out + v[:, q * V7X_LANES:(q + 1) * V7X_LANES]
    return out


def _inv_rms(part, d):
    return lax.rsqrt(jnp.sum(part, axis=-1, keepdims=True) / d + EPS)


def _outmm_kernel(ya_ref, yb_ref, wa_ref, wb_ref, o_ref):
    o_ref[...] = (jnp.dot(ya_ref[...], wa_ref[...], preferred_element_type=F32)
                  + jnp.dot(yb_ref[...], wb_ref[...], preferred_element_type=F32))


def _outmm(ya, yb, w_out, tm=1024, tn=1024):
    t, aw = ya.shape
    d = w_out.shape[1]
    return pl.pallas_call(
        _outmm_kernel,
        grid=(t // tm, d // tn),
        in_specs=[pl.BlockSpec((tm, aw), lambda i, j: (i, 0)),
                  pl.BlockSpec((tm, aw), lambda i, j: (i, 0)),
                  pl.BlockSpec((aw, tn), lambda i, j: (0, j)),
                  pl.BlockSpec((aw, tn), lambda i, j: (1, j))],
        out_specs=pl.BlockSpec((tm, tn), lambda i, j: (i, j)),
        out_shape=jax.ShapeDtypeStruct((t, d), F32),
        compiler_params=_params("arbitrary", "arbitrary"),
        name="outmm",
    )(ya, yb, w_out, w_out)


def _postmix_kernel(y_ref, x_ref, gpost_ref, gate_ref, gpre_ref, sc_ref, sh_ref, wr_ref,
                    x1_ref, h2b_ref, h2p_ref, lg_ref, h2lo_ref, vec_ref):
    tm, d = x_ref.shape
    nr, pitch = _slab_rows(d), _slab_pitch(d)
    vec_ref[0:1, :] = gpost_ref[...] * gate_ref[0]
    vec_ref[1:2, :] = gpre_ref[...] * (1.0 + sc_ref[0])
    vec_ref[2:3, :] = sh_ref[0]

    def rows_of(r, u):
        return pl.multiple_of((r * NORM_UNROLL + u) * NORM_ROWS, NORM_ROWS)

    def body(r, carry):
        groups = [rows_of(r, u) for u in range(NORM_UNROLL)]
        invs = []
        for r0 in groups:
            rows = pl.ds(r0, NORM_ROWS)
            part = jnp.zeros((NORM_ROWS, V7X_LANES), F32)
            for c0 in range(0, d, NORM_COLS):
                y = y_ref[rows, c0:c0 + NORM_COLS]
                part = part + _fold_lanes(y * y)
            invs.append(_inv_rms(part, d))
        invs2 = []
        for r0, inv in zip(groups, invs):
            rows = pl.ds(r0, NORM_ROWS)
            part = jnp.zeros((NORM_ROWS, V7X_LANES), F32)
            for c0 in range(0, d, NORM_COLS):
                cols = slice(c0, c0 + NORM_COLS)
                x1 = x_ref[rows, cols] + (y_ref[rows, cols] * inv) * vec_ref[0:1, cols]
                x1_ref[rows, cols] = x1
                part = part + _fold_lanes(x1 * x1)
            invs2.append(_inv_rms(part, d))
        for r0, inv in zip(groups, invs2):
            rows = pl.ds(r0, NORM_ROWS)
            for c0 in range(0, d // 2, NORM_COLS):
                halves = []
                for cols in (slice(c0, c0 + NORM_COLS),
                             slice(d // 2 + c0, d // 2 + c0 + NORM_COLS)):
                    h2 = (x1_ref[rows, cols] * inv) * vec_ref[1:2, cols] + vec_ref[2:3, cols]
                    hb = h2.astype(BF16)
                    h2b_ref[rows, cols] = hb
                    hb32 = hb.astype(F32)
                    h2lo_ref[rows, cols] = (h2 - hb32).astype(BF16)
                    halves.append(hb32)
                words = _pack_pair(*halves)
                for q in range(NORM_COLS // V7X_LANES):
                    s = c0 // V7X_LANES + q
                    h2p_ref[pl.ds(r0 * pitch + s, NORM_ROWS, stride=pitch), :] = (
                        words[:, q * V7X_LANES:(q + 1) * V7X_LANES])
            for s in range(nr, pitch):
                h2p_ref[pl.ds(r0 * pitch + s, NORM_ROWS, stride=pitch), :] = (
                    jnp.zeros((NORM_ROWS, V7X_LANES), U32))
        return carry

    lax.fori_loop(0, tm // (NORM_ROWS * NORM_UNROLL), body, 0)
    w_hi, w_lo = _split_bf16(wr_ref[...], 2)
    p = jnp.dot(h2b_ref[...], jnp.concatenate([w_hi, w_lo], axis=1),
                preferred_element_type=F32)
    p_lo = jnp.dot(h2lo_ref[...], w_hi, preferred_element_type=F32)
    lg_ref[...] = p[:, :V7X_LANES] + p[:, V7X_LANES:] + p_lo


def _postmix(ymix, x2, gpost, mod3, gpre, w_rp, seq, tm=256):
    t, d = x2.shape
    per_b = seq // tm
    pitch = _slab_pitch(d)
    modspec = lambda m: pl.BlockSpec((1, 1, d), lambda i: ((i // per_b) * N_MOD + m, 0, 0))
    vecspec = pl.BlockSpec((1, d), lambda i: (0, 0))
    return pl.pallas_call(
        _postmix_kernel,
        grid=(t // tm,),
        in_specs=[pl.BlockSpec((tm, d), lambda i: (i, 0)),
                  pl.BlockSpec((tm, d), lambda i: (i, 0)),
                  vecspec, modspec(2), vecspec, modspec(4), modspec(3),
                  pl.BlockSpec((d, V7X_LANES), lambda i: (0, 0))],
        out_specs=[pl.BlockSpec((tm, d), lambda i: (i, 0)),
                   pl.BlockSpec((tm, d), lambda i: (i, 0)),
                   pl.BlockSpec((tm * pitch, V7X_LANES), lambda i: (i, 0)),
                   pl.BlockSpec((tm, V7X_LANES), lambda i: (i, 0))],
        out_shape=[jax.ShapeDtypeStruct((t, d), F32),
                   jax.ShapeDtypeStruct((t, d), BF16),
                   jax.ShapeDtypeStruct((t * pitch, V7X_LANES), U32),
                   jax.ShapeDtypeStruct((t, V7X_LANES), F32)],
        scratch_shapes=[pltpu.VMEM((tm, d), BF16),
                        pltpu.VMEM((8, d), F32)],
        compiler_params=_params("arbitrary"),
        name="postmix",
    )(ymix, x2, gpost, mod3, gpre, mod3, mod3, w_rp)


def _first_argmax(x, iota, axis, size):
    m = jnp.max(x, axis=axis, keepdims=True)
    idx = jnp.min(jnp.where(x == m, iota, size), axis=axis, keepdims=True)
    return m, idx


def _route_kernel(lg_ref, bias_ref, idx_ref, w_ref, rank_ref, cnt_ref, seen_ref):
    tt = lg_ref.shape[0]
    ne, ng = N_EXPERTS, N_GROUPS

    @pl.when(pl.program_id(0) == 0)
    def _():
        seen_ref[...] = jnp.zeros_like(seen_ref)

    gs = ne // ng
    neg = -jnp.inf
    scores = jax.nn.sigmoid(lg_ref[...].T[:ne, :])
    sel = scores + bias_ref[...]
    g3 = sel.reshape(ng, gs, tt)
    j_iota = lax.broadcasted_iota(jnp.int32, (ng, gs, tt), 1)
    m1, i1 = _first_argmax(g3, j_iota, 1, gs)
    m2 = jnp.max(jnp.where(j_iota == i1, neg, g3), axis=1, keepdims=True)
    grp = (m1 + m2).reshape(ng, tt)
    g_iota = lax.broadcasted_iota(jnp.int32, (ng, tt), 0)
    keep = jnp.zeros((ng, tt), jnp.bool_)
    for _ in range(TOPK_GROUPS):
        _, gi = _first_argmax(grp, g_iota, 0, ng)
        hit = g_iota == gi
        keep = jnp.logical_or(keep, hit)
        grp = jnp.where(hit, neg, grp)
    keep3 = jnp.broadcast_to(keep.reshape(ng, 1, tt), (ng, gs, tt))
    cand = jnp.where(keep3, g3, neg).reshape(ne, tt)
    e_iota = lax.broadcasted_iota(jnp.int32, (ne, tt), 0)
    idxs, ws, hits = [], [], []
    for _ in range(TOP_K):
        _, ei = _first_argmax(cand, e_iota, 0, ne)
        hit = e_iota == ei
        idxs.append(ei)
        hits.append(hit)
        ws.append(jnp.sum(jnp.where(hit, scores, 0.0), axis=0, keepdims=True))
        cand = jnp.where(hit, neg, cand)
    total = ws[0]
    for w in ws[1:]:
        total = total + w
    pad = idx_ref.shape[0] - TOP_K
    idx_ref[...] = jnp.concatenate(idxs + [jnp.zeros((pad, tt), jnp.int32)], axis=0)
    w_ref[...] = jnp.concatenate([w / total * ROUTED_SCALE for w in ws]
                                 + [jnp.zeros((pad, tt), F32)], axis=0)

    chosen = hits[0]
    for hit in hits[1:]:
        chosen = jnp.logical_or(chosen, hit)
    chosen = jnp.where(chosen, 1.0, 0.0)
    src = lax.broadcasted_iota(jnp.int32, (tt, tt), 0)
    dst = lax.broadcasted_iota(jnp.int32, (tt, tt), 1)
    before = jnp.where(src < dst, 1.0, 0.0).astype(BF16)
    rank_e = seen_ref[:, 0:1] + jnp.dot(chosen.astype(BF16), before,
                                        preferred_element_type=F32)
    ranks = [jnp.sum(jnp.where(hit, rank_e, 0.0), axis=0, keepdims=True) for hit in hits]
    rank_ref[...] = jnp.concatenate(ranks + [jnp.zeros((pad, tt), F32)],
                                    axis=0).astype(jnp.int32)
    seen_ref[...] = seen_ref[...] + jnp.sum(chosen, axis=1, keepdims=True)
    cnt_ref[...] = seen_ref[...].astype(jnp.int32)


def _route(logits, bias, tt=512):
    t = logits.shape[0]
    tok_spec = pl.BlockSpec((8, tt), lambda i: (0, i))
    return pl.pallas_call(
        _route_kernel,
        grid=(t // tt,),
        in_specs=[pl.BlockSpec((tt, V7X_LANES), lambda i: (i, 0)),
                  pl.BlockSpec((N_EXPERTS, 1), lambda i: (0, 0))],
        out_specs=[tok_spec, tok_spec, tok_spec,
                   pl.BlockSpec((N_EXPERTS, V7X_LANES), lambda i: (0, 0))],
        out_shape=[jax.ShapeDtypeStruct((8, t), jnp.int32),
                   jax.ShapeDtypeStruct((8, t), F32),
                   jax.ShapeDtypeStruct((8, t), jnp.int32),
                   jax.ShapeDtypeStruct((N_EXPERTS, V7X_LANES), jnp.int32)],
        scratch_shapes=[pltpu.VMEM((N_EXPERTS, V7X_LANES), F32)],
        compiler_params=_params("arbitrary"),
        name="route",
    )(logits, bias.reshape(N_EXPERTS, 1))


def _plan_kernel(idx_ref, rank_ref, cnt_ref, dest_ref, be_ref, nr_ref, tgt_ref, ws_ref,
                 *, bm, nch):
    ne = N_EXPERTS
    shift = bm.bit_length() - 1
    counts = cnt_ref[...]
    padded = ((counts + (bm - 1)) >> shift) << shift
    r = lax.broadcasted_iota(jnp.int32, (ne, ne), 0)
    c = lax.broadcasted_iota(jnp.int32, (ne, ne), 1)
    upto = jnp.where(c <= r, 1.0, 0.0)
    pend = jnp.dot(upto, padded.astype(F32), precision=HIGHEST,
                   preferred_element_type=F32).astype(jnp.int32)
    pstart = pend - padded
    idx = idx_ref[...]
    dest = rank_ref[...]
    blk = lax.broadcasted_iota(jnp.int32, be_ref.shape, 1)
    blk_row0 = blk * bm
    blk_e = jnp.zeros(be_ref.shape, jnp.int32)
    lane_e = lax.broadcasted_iota(jnp.int32, ws_ref.shape, 1)
    used_before = jnp.zeros(ws_ref.shape, jnp.int32)
    for e in range(ne):
        dest = dest + jnp.where(idx == e, pstart[e:e + 1, 0:1], 0)
        blk_e = blk_e + jnp.where(pend[e:e + 1, 0:1] <= blk_row0, 1, 0)
        used_before = used_before + jnp.where(
            lane_e > e, jnp.where(counts[e:e + 1, 0:1] > 0, 1, 0), 0)
    dest_ref[...] = dest
    blk_e = jnp.minimum(blk_e, ne - 1)
    be_ref[...] = blk_e
    nr_ref[...] = jnp.broadcast_to(pend[ne - 1:ne, :] >> shift, nr_ref.shape)
    ws_ref[...] = jnp.bitwise_and(used_before, 1)

    seg_end = jnp.zeros(be_ref.shape, jnp.int32)
    seg_start = jnp.zeros(be_ref.shape, jnp.int32)
    for e in range(ne):
        mine = blk_e == e
        seg_end = seg_end + jnp.where(mine, pend[e:e + 1, 0:1], 0)
        seg_start = seg_start + jnp.where(mine, pstart[e:e + 1, 0:1], 0)
    nxt = jnp.zeros(be_ref.shape, jnp.int32)
    for e in range(ne):
        nxt = nxt + jnp.where(pend[e:e + 1, 0:1] <= seg_end, 1, 0)
    need = jnp.where(nxt < ne, (nxt - blk_e) * nch, 0)
    n_blk = jnp.maximum((seg_end - seg_start) >> shift, 1)
    j = blk - (seg_start >> shift)
    per = (need.astype(F32) / n_blk.astype(F32)).astype(jnp.int32)
    per = per + jnp.where((per + 1) * n_blk <= need, 1, 0)
    per = per - jnp.where(per * n_blk > need, 1, 0)
    per = per + jnp.where(per * n_blk < need, 1, 0)
    tgt_ref[...] = (blk_e + 1) * nch + jnp.minimum(need, (j + 1) * per)


def _plan(idx8, rank8, cnt, bm, nb, nch, tt=2048):
    t = idx8.shape[1]
    tt = min(tt, t)
    nbp = -(-nb // V7X_LANES) * V7X_LANES
    tok_spec = pl.BlockSpec((8, tt), lambda i: (0, i))
    blk_spec = pl.BlockSpec((8, nbp), lambda i: (0, 0))
    one_spec = pl.BlockSpec((8, V7X_LANES), lambda i: (0, 0))
    return pl.pallas_call(
        functools.partial(_plan_kernel, bm=bm, nch=nch),
        grid=(t // tt,),
        in_specs=[tok_spec, tok_spec,
                  pl.BlockSpec((N_EXPERTS, V7X_LANES), lambda i: (0, 0))],
        out_specs=[tok_spec, blk_spec, one_spec, blk_spec, one_spec],
        out_shape=[jax.ShapeDtypeStruct((8, t), jnp.int32),
                   jax.ShapeDtypeStruct((8, nbp), jnp.int32),
                   jax.ShapeDtypeStruct((8, V7X_LANES), jnp.int32),
                   jax.ShapeDtypeStruct((8, nbp), jnp.int32),
                   jax.ShapeDtypeStruct((8, V7X_LANES), jnp.int32)],
        compiler_params=_params("arbitrary"),
        name="plan",
    )(idx8, rank8, cnt)


def _row_gather(src_hbm, dst, sem, src_tok, dst_tok, nr, pitch):
    return pltpu.make_async_copy(src_hbm.at[pl.ds(src_tok * pitch, nr)],
                                 dst.at[pl.ds(dst_tok * pitch, nr)], sem)


def _gather_wait(src_hbm, dst, sem, n_tok, nr):
    pltpu.make_async_copy(src_hbm.at[pl.ds(0, n_tok * nr)],
                          dst.at[pl.ds(0, n_tok * nr)], sem).wait()


WCHUNK_ROWS = 512


def _wchunks(d):
    each = d // WCHUNK_ROWS
    return each, 3 * each


def _experts_kernel(be_ref, nr_ref, st_ref, tgt_ref, ws_ref,
                    h2s_hbm, wg_hbm, wu_hbm, wd_hbm, y_ref,
                    xbuf, sem, x_ref, wg_b, wu_b, wd_b, stg_a, stg_b, wsem, done_ref):
    i = pl.program_id(0)
    n_real = nr_ref[0]
    _, d, f = wg_b.shape
    nr, pitch = _slab_rows(d), _slab_pitch(d)
    bm = xbuf.shape[1] // pitch
    each, nch = _wchunks(d)
    dn_rows = f // each
    slot = lax.rem(i, 2)
    e_cur = be_ref[jnp.maximum(jnp.minimum(i, n_real - 1), 0)]
    limit = (be_ref[jnp.maximum(n_real - 1, 0)] + 1) * nch

    def start_gather(blk, sl):
        def body(r, carry):
            tok = st_ref[blk * bm + r]
            _row_gather(h2s_hbm, xbuf.at[sl], sem.at[sl], tok, r, nr, pitch).start()
            return carry
        lax.fori_loop(0, bm, body, 0, unroll=8)

    def chunk_parts(g):
        ex = g // nch
        c = g - ex * nch
        return ex, c

    def chunk_rows(kind, c):
        n = dn_rows if kind == 2 else WCHUNK_ROWS
        return pl.ds(lax.rem(c, each) * n, n)

    def chunk_copy(kind, ex, c, sl):
        src = (wg_hbm, wu_hbm, wd_hbm)[kind]
        stg = stg_b if kind == 2 else stg_a
        return pltpu.make_async_copy(src.at[ex, chunk_rows(kind, c), :], stg.at[sl],
                                     wsem.at[sl])

    def chunk_start(g, sl):
        ex, c = chunk_parts(g)
        for kind in range(3):
            @pl.when(c // each == kind)
            def _():
                chunk_copy(kind, ex, c, sl).start()

    def chunk_finish(g, sl):
        ex, c = chunk_parts(g)
        wsl = ws_ref[ex]
        for kind in range(3):
            @pl.when(c // each == kind)
            def _():
                chunk_copy(kind, ex, c, sl).wait()
                stg = stg_b if kind == 2 else stg_a
                dst = (wg_b, wu_b, wd_b)[kind]
                dst[wsl, chunk_rows(kind, c), :] = stg[sl].astype(BF16)

    def convert_until(target):
        def body(g, carry):
            sl = lax.rem(g, 2)
            chunk_finish(g, sl)

            @pl.when(g + 2 < limit)
            def _():
                chunk_start(g + 2, sl)
            return carry
        lax.fori_loop(done_ref[0], target, body, 0)
        done_ref[0] = jnp.maximum(done_ref[0], target)

    @pl.when(i == 0)
    def _():
        done_ref[0] = 0
        chunk_start(0, 0)
        chunk_start(1, 1)
        start_gather(0, 0)

    @pl.when(i + 1 < n_real)
    def _():
        start_gather(i + 1, 1 - slot)

    @pl.when(i < n_real)
    def _():
        convert_until((e_cur + 1) * nch)
        wsl = ws_ref[e_cur]
        _gather_wait(h2s_hbm, xbuf.at[slot], sem.at[slot], bm, nr)
        for kc in range(nr // SLAB_GROUP):
            lo, hi = _unpack_pair(_slab_cols(xbuf.at[slot], 0, bm, pitch, kc))
            x_ref[:, kc * SLAB_K:(kc + 1) * SLAB_K] = lo.astype(BF16)
            x_ref[:, d // 2 + kc * SLAB_K:d // 2 + (kc + 1) * SLAB_K] = hi.astype(BF16)
        x = x_ref[...]
        g = jnp.dot(x, wg_b[wsl], preferred_element_type=F32)
        u = jnp.dot(x, wu_b[wsl], preferred_element_type=F32)
        a = (_silu(g) * u).astype(BF16)
        for kc in range(nr // SLAB_GROUP):
            halves = [_round_bf16(jnp.dot(a, wd_b[wsl, :, c0:c0 + SLAB_K],
                                          preferred_element_type=F32))
                      for c0 in (kc * SLAB_K, d // 2 + kc * SLAB_K)]
            words = _pack_pair(*halves)
            for q in range(SLAB_GROUP):
                y_ref[pl.ds(kc * SLAB_GROUP + q, bm, stride=pitch), :] = (
                    words[:, q * V7X_LANES:(q + 1) * V7X_LANES])
        for s in range(nr, pitch):
            y_ref[pl.ds(s, bm, stride=pitch), :] = jnp.zeros((bm, V7X_LANES), U32)
        convert_until(tgt_ref[i])

    @pl.when(i >= n_real)
    def _():
        y_ref[...] = jnp.zeros_like(y_ref)


def _experts(block_e, n_real, slot_tok, tgt, wslot, h2s, wg, wu, wd, bm):
    nb = block_e.shape[0]
    ne, d, f = wg.shape
    pitch = _slab_pitch(d)
    each, _ = _wchunks(d)
    hbm = pl.BlockSpec(memory_space=pl.ANY)
    return pl.pallas_call(
        _experts_kernel,
        grid_spec=pltpu.PrefetchScalarGridSpec(
            num_scalar_prefetch=5,
            grid=(nb,),
            in_specs=[hbm, hbm, hbm, hbm],
            out_specs=pl.BlockSpec((bm * pitch, V7X_LANES), lambda i, *_: (i, 0)),
            scratch_shapes=[pltpu.VMEM((2, bm * pitch, V7X_LANES), U32),
                            pltpu.SemaphoreType.DMA((2,)),
                            pltpu.VMEM((bm, d), BF16),
                            pltpu.VMEM((2, d, f), BF16),
                            pltpu.VMEM((2, d, f), BF16),
                            pltpu.VMEM((2, f, d), BF16),
                            pltpu.VMEM((2, WCHUNK_ROWS, f), F32),
                            pltpu.VMEM((2, f // each, d), F32),
                            pltpu.SemaphoreType.DMA((2,)),
                            pltpu.SMEM((1,), jnp.int32)]),
        out_shape=jax.ShapeDtypeStruct((nb * bm * pitch, V7X_LANES), U32),
        compiler_params=_params("arbitrary"),
        name="experts",
    )(block_e, n_real, slot_tok, tgt, wslot, h2s, wg, wu, wd)


def _shared_kernel(cnt_ref, dest_ref, h_ref, wg_ref, wu_ref, wd_ref, o_ref, st_ref, *, bm):
    i = pl.program_id(0)
    tm = h_ref.shape[0]
    ns = st_ref.shape[0]

    @pl.when(i == 0)
    def _():
        def clear(lo, hi):
            def body(s, carry):
                st_ref[s] = 0
                return carry
            lax.fori_loop(lo, hi, body, 0)

        def per_expert(e, seg_start):
            n = cnt_ref[e]
            seg_end = seg_start + (n + (bm - 1)) // bm * bm
            clear(seg_start + n, seg_end)
            return seg_end

        used = lax.fori_loop(0, N_EXPERTS, per_expert, 0)
        clear(used, ns)

    for t in range(tm):
        for k in range(TOP_K):
            st_ref[dest_ref[k * tm + t]] = i * tm + t

    h = h_ref[...]
    g = jnp.dot(h, wg_ref[...], preferred_element_type=F32)
    u = jnp.dot(h, wu_ref[...], preferred_element_type=F32)
    a = (_silu(g) * u).astype(BF16)
    o_ref[...] = jnp.dot(a, wd_ref[...], preferred_element_type=F32)


def _shared(counts, dest8, h2b, wg, wu, wd, bm, nb, tm=256):
    t, d = h2b.shape
    f = wg.shape[1]
    rows = dest8.shape[0]
    dest_tiles = dest8.reshape(rows, t // tm, tm).transpose(1, 0, 2).reshape(-1)
    return pl.pallas_call(
        functools.partial(_shared_kernel, bm=bm),
        grid=(t // tm,),
        in_specs=[pl.BlockSpec(memory_space=pltpu.SMEM),
                  pl.BlockSpec((rows * tm,), lambda i: (i,), memory_space=pltpu.SMEM),
                  pl.BlockSpec((tm, d), lambda i: (i, 0)),
                  pl.BlockSpec((d, f), lambda i: (0, 0)),
                  pl.BlockSpec((d, f), lambda i: (0, 0)),
                  pl.BlockSpec((f, d), lambda i: (0, 0))],
        out_specs=[pl.BlockSpec((tm, d), lambda i: (i, 0)),
                   pl.BlockSpec(memory_space=pltpu.SMEM)],
        out_shape=[jax.ShapeDtypeStruct((t, d), F32),
                   jax.ShapeDtypeStruct((nb * bm,), jnp.int32)],
        compiler_params=_params("arbitrary"),
        name="shared",
    )(counts, dest_tiles, h2b, wg, wu, wd)


def _combine_kernel(dest_ref, ys_hbm, w_ref, sh_ref, x1_ref, g_ref, gate_ref,
                    o_ref, gbuf0, gbuf1, ybuf, vec_ref, sem):
    gbufs = (gbuf0, gbuf1)
    i = pl.program_id(0)
    n = pl.num_programs(0)
    tt, d = x1_ref.shape
    nr, pitch = _slab_rows(d), _slab_pitch(d)
    slot = lax.rem(i, 2)

    def gather_row(blk, buf, r, k):
        row = dest_ref[k * (n * tt) + blk * tt + r]
        _row_gather(ys_hbm, gbufs[buf], sem.at[buf], row, k * tt + r, nr, pitch).start()

    @pl.when(i == 0)
    def _():
        def body(r, carry):
            for k in range(TOP_K):
                gather_row(0, 0, r, k)
            return carry
        lax.fori_loop(0, tt, body, 0, unroll=4)

    def weighted_sum(cur):
        nxt = 1 - cur
        _gather_wait(ys_hbm, gbufs[cur], sem.at[cur], TOP_K * tt, nr)
        nxt_blk = jnp.minimum(i + 1, n - 1)
        for r in range(tt):
            for k in range(TOP_K):
                gather_row(nxt_blk, nxt, r, k)
        wk = [jnp.broadcast_to(w_ref[:, k:k + 1], (tt, V7X_LANES)) for k in range(TOP_K)]
        for s in range(nr):
            cols_lo = slice(s * V7X_LANES, (s + 1) * V7X_LANES)
            cols_hi = slice(d // 2 + s * V7X_LANES, d // 2 + (s + 1) * V7X_LANES)
            acc_lo = sh_ref[:, cols_lo]
            acc_hi = sh_ref[:, cols_hi]
            for k in range(TOP_K):
                lo, hi = _unpack_pair(
                    gbufs[cur][pl.ds(k * tt * pitch + s, tt, stride=pitch), :])
                acc_lo = acc_lo + lo * wk[k]
                acc_hi = acc_hi + hi * wk[k]
            ybuf[:, cols_lo] = acc_lo
            ybuf[:, cols_hi] = acc_hi

        @pl.when(i + 1 == n)
        def _():
            _gather_wait(ys_hbm, gbufs[nxt], sem.at[nxt], TOP_K * tt, nr)

    for cur in range(2):
        @pl.when(slot == cur)
        def _():
            weighted_sum(cur)

    vec_ref[0:1, :] = g_ref[...] * gate_ref[0]

    def body(r, carry):
        groups = [pl.ds(pl.multiple_of((r * NORM_UNROLL + u) * NORM_ROWS, NORM_ROWS),
                        NORM_ROWS) for u in range(NORM_UNROLL)]
        invs = []
        for rows in groups:
            part = jnp.zeros((NORM_ROWS, V7X_LANES), F32)
            for c0 in range(0, d, NORM_COLS):
                yc = ybuf[rows, c0:c0 + NORM_COLS]
                part = part + _fold_lanes(yc * yc)
            invs.append(_inv_rms(part, d))
        for rows, inv in zip(groups, invs):
            for c0 in range(0, d, NORM_COLS):
                cols = slice(c0, c0 + NORM_COLS)
                o_ref[rows, cols] = (x1_ref[rows, cols]
                                     + (ybuf[rows, cols] * inv) * vec_ref[0:1, cols])
        return carry

    lax.fori_loop(0, tt // (NORM_ROWS * NORM_UNROLL), body, 0)


def _combine(dest, y_rows, wts, shared, x1, g, mod3, seq, tt=128):
    t, d = x1.shape
    pitch = _slab_pitch(d)
    per_b = seq // tt
    return pl.pallas_call(
        _combine_kernel,
        grid_spec=pltpu.PrefetchScalarGridSpec(
            num_scalar_prefetch=1,
            grid=(t // tt,),
            in_specs=[pl.BlockSpec(memory_space=pl.ANY),
                      pl.BlockSpec((tt, 8), lambda i, ds: (i, 0)),
                      pl.BlockSpec((tt, d), lambda i, ds: (i, 0)),
                      pl.BlockSpec((tt, d), lambda i, ds: (i, 0)),
                      pl.BlockSpec((1, d), lambda i, ds: (0, 0)),
                      pl.BlockSpec((1, 1, d), lambda i, ds: ((i // per_b) * N_MOD + 5, 0, 0))],
            out_specs=pl.BlockSpec((tt, d), lambda i, ds: (i, 0)),
            scratch_shapes=[pltpu.VMEM((TOP_K * tt * pitch, V7X_LANES), U32),
                            pltpu.VMEM((TOP_K * tt * pitch, V7X_LANES), U32),
                            pltpu.VMEM((tt, d), F32),
                            pltpu.VMEM((8, d), F32),
                            pltpu.SemaphoreType.DMA((2,))]),
        out_shape=jax.ShapeDtypeStruct((t, d), F32),
        compiler_params=_params("arbitrary"),
        name="combine",
    )(dest, y_rows, wts, shared, x1, g, mod3)


EXPERT_BLOCK_ROWS = 256


def kernel(x, c, w_ada, b_ada, g_pre_mix, g_post_mix, g_pre_ffn, g_post_ffn, w_in,
           a_ln_g, a_ln_b, a_w_s, a_b_s, a_out_g, b_w_g2, b_b_g2, b_head_g, w_out,
           w_router, router_bias, we_gate, we_up, we_down, ws_gate, ws_up, ws_down):
    bsz, seq, d = x.shape
    t = bsz * seq
    aw = d // 2
    vw = d - aw
    kw = vw // 2
    n_main = 2 * aw + 2 * kw + 2 * vw
    x2 = x.reshape(t, d)
    for l in range(w_ada.shape[0]):
        mod = _ada(c, w_ada[l], b_ada[l])
        mod3 = mod.reshape(bsz * N_MOD, 1, d)

        w_gl = jnp.pad(w_in[l][:, n_main:], ((0, 0), (0, V7X_LANES - B_GATE_RANK))).astype(BF16)
        proj, g_low = _inproj(x2, g_pre_mix[l][None], mod3, w_in[l].astype(BF16), n_main,
                              w_gl, seq)
        y_a = _mixer_a(proj, a_ln_g[l][None], a_ln_b[l][None], a_w_s[l], a_b_s[l],
                       a_out_g[l][None], aw)
        w_g2p = jnp.pad(b_w_g2[l], ((0, V7X_LANES - B_GATE_RANK), (0, 0)))
        y_b = _mixer_b(proj, g_low, w_g2p, b_b_g2[l][None], b_head_g[l][None],
                       bsz, seq, kw, vw)

        w_rp = jnp.pad(w_router[l], ((0, 0), (0, V7X_LANES - N_EXPERTS)))
        ymix = _outmm(y_a, y_b, w_out[l].astype(BF16))
        x1, h2b, h2s, logits = _postmix(ymix, x2, g_post_mix[l][None], mod3,
                                        g_pre_ffn[l][None], w_rp, seq)
        idx8, w8, rank8, cnt = _route(logits, router_bias[l])
        bm = EXPERT_BLOCK_ROWS
        nb = -(-t * TOP_K // bm) + N_EXPERTS
        dest8, be8, nr8, tgt8, ws8 = _plan(idx8, rank8, cnt, bm, nb, _wchunks(d)[1])
        shared, slot_tok = _shared(cnt[:, 0], dest8, h2b, ws_gate[l].astype(BF16),
                                   ws_up[l].astype(BF16), ws_down[l].astype(BF16), bm, nb)
        y_rows = _experts(be8[0, :nb], nr8[0, :1], slot_tok, tgt8[0, :nb], ws8[0, :N_EXPERTS],
                          h2s, we_gate[l], we_up[l], we_down[l], bm)
        x2 = _combine(dest8[:TOP_K].reshape(-1), y_rows, w8.T, shared, x1, g_post_ffn[l][None],
                      mod3, seq)
    return x2.reshape(bsz, seq, d)
```

```python
import functools

import jax
import jax.numpy as jnp
from jax import lax
from jax.experimental import pallas as pl
from jax.experimental.pallas import tpu as pltpu

F32 = jnp.float32
BF16 = jnp.bfloat16
HIGHEST = lax.Precision.HIGHEST

V7X_LANES = 128
V7X_SUBLANES = 8
V7X_VMEM_LIMIT_BYTES = 58 * 1024 * 1024

A_HEADS = 8
A_CHUNK = 128
B_HEADS = 4
B_GATE_RANK = 16
B_GATE_TAU = 16.0
B_CHUNK = 64
N_EXPERTS = 64
TOP_K = 6
N_GROUPS = 8
TOPK_GROUPS = 4
ROUTED_SCALE = 2.5
N_MOD = 6
EPS = 1e-6

SLAB_PAD = 4
U32 = jnp.uint32
HIGH_HALF = 0xFFFF0000


def _slab_rows(d):
    return d // (2 * V7X_LANES)


def _slab_pitch(d):
    return _slab_rows(d) + SLAB_PAD


def _round_bf16(x):
    return x.astype(BF16).astype(F32)


def _pack_pair(lo, hi):
    lo_bits = lax.bitcast_convert_type(lo, U32)
    hi_bits = lax.bitcast_convert_type(hi, U32)
    return jnp.bitwise_or(jnp.bitwise_and(hi_bits, U32(HIGH_HALF)),
                          jnp.right_shift(lo_bits, U32(16)))


def _unpack_pair(w):
    lo = lax.bitcast_convert_type(jnp.left_shift(w, U32(16)), F32)
    hi = lax.bitcast_convert_type(jnp.bitwise_and(w, U32(HIGH_HALF)), F32)
    return lo, hi


SLAB_GROUP = 4
SLAB_K = SLAB_GROUP * V7X_LANES


def _slab_cols(ref, first_tok, n_tok, pitch, kc):
    return jnp.concatenate(
        [ref[pl.ds(first_tok * pitch + kc * SLAB_GROUP + q, n_tok, stride=pitch), :]
         for q in range(SLAB_GROUP)], axis=-1)


def _params(*sem):
    return pltpu.CompilerParams(dimension_semantics=sem,
                                vmem_limit_bytes=V7X_VMEM_LIMIT_BYTES)


def _silu(x):
    return x * jax.nn.sigmoid(x)


ADA_ROWS = 64


def _ada_kernel(c_ref, w_ref, b_ref, o_ref, cb_ref):
    d, nb = c_ref.shape
    tn = w_ref.shape[1]

    @pl.when(pl.program_id(0) == 0)
    def _():
        def fill(i, carry):
            rows = pl.ds(pl.multiple_of(i * ADA_ROWS, ADA_ROWS), ADA_ROWS)
            cc = _silu(c_ref[rows, :])
            for b in range(nb):
                cb_ref[b, rows, :] = jnp.broadcast_to(cc[:, b:b + 1], (ADA_ROWS, V7X_LANES))
            return carry
        lax.fori_loop(0, d // ADA_ROWS, fill, 0)

    def body(i, accs):
        r0 = pl.multiple_of(i * ADA_ROWS, ADA_ROWS)
        w = w_ref[pl.ds(r0, ADA_ROWS), :]
        out = []
        for b in range(nb):
            cb = cb_ref[b, pl.ds(r0, ADA_ROWS), :]
            cols = []
            for q in range(tn // V7X_LANES):
                p = w[:, q * V7X_LANES:(q + 1) * V7X_LANES] * cb
                cols.append(p.reshape(ADA_ROWS // 8, 8, V7X_LANES).sum(axis=0))
            out.append(accs[b] + jnp.concatenate(cols, axis=-1))
        return tuple(out)

    accs = lax.fori_loop(0, d // ADA_ROWS, body,
                         tuple(jnp.zeros((8, tn), F32) for _ in range(nb)))
    for b in range(nb):
        o_ref[b:b + 1, :] = accs[b].sum(axis=0, keepdims=True) + b_ref[...]


def _ada(c, w_ada, b_ada, tn=1024):
    nb, d = c.shape
    n = w_ada.shape[1]
    return pl.pallas_call(
        _ada_kernel,
        grid=(n // tn,),
        in_specs=[pl.BlockSpec((d, nb), lambda j: (0, 0)),
                  pl.BlockSpec((d, tn), lambda j: (0, j)),
                  pl.BlockSpec((1, tn), lambda j: (0, j))],
        out_specs=pl.BlockSpec((nb, tn), lambda j: (0, j)),
        out_shape=jax.ShapeDtypeStruct((nb, n), F32),
        scratch_shapes=[pltpu.VMEM((nb, d, V7X_LANES), F32)],
        compiler_params=_params("arbitrary"),
        name="ada",
    )(c.T, w_ada, b_ada.reshape(1, n))


NORM_ROWS = 16
NORM_UNROLL = 2


def _rms_rows(x, g):
    ms = jnp.mean(x * x, axis=-1, keepdims=True)
    return x * lax.rsqrt(ms + EPS) * g


NORM_COLS = 512


def _inproj_kernel(x_ref, g_ref, sc_ref, sh_ref, w_ref, wgl_ref, o_ref, gl_ref, h_ref,
                   vec_ref):
    tm, d = x_ref.shape

    @pl.when(pl.program_id(1) == 0)
    def _():
        vec_ref[0:1, :] = g_ref[...] * (1.0 + sc_ref[0])
        vec_ref[1:2, :] = sh_ref[0]

        def body(r, carry):
            groups = [pl.ds(pl.multiple_of((r * NORM_UNROLL + u) * NORM_ROWS, NORM_ROWS),
                            NORM_ROWS) for u in range(NORM_UNROLL)]
            invs = []
            for rows in groups:
                part = jnp.zeros((NORM_ROWS, V7X_LANES), F32)
                for c0 in range(0, d, NORM_COLS):
                    xc = x_ref[rows, c0:c0 + NORM_COLS]
                    part = part + _fold_lanes(xc * xc)
                invs.append(_inv_rms(part, d))
            for rows, inv in zip(groups, invs):
                for c0 in range(0, d, NORM_COLS):
                    cols = slice(c0, c0 + NORM_COLS)
                    h = (x_ref[rows, cols] * inv) * vec_ref[0:1, cols] + vec_ref[1:2, cols]
                    h_ref[rows, cols] = h.astype(BF16)
            return carry

        lax.fori_loop(0, tm // (NORM_ROWS * NORM_UNROLL), body, 0)
        gl_ref[...] = jnp.dot(h_ref[...], wgl_ref[...], preferred_element_type=F32)

    o_ref[...] = jnp.dot(h_ref[...], w_ref[...],
                         preferred_element_type=F32).astype(o_ref.dtype)


def _inproj(x2, g, mod3, w_all, n, w_gl, seq, tm=512, tn=1024):
    t, d = x2.shape
    per_b = seq // tm
    return pl.pallas_call(
        _inproj_kernel,
        grid=(t // tm, n // tn),
        in_specs=[pl.BlockSpec((tm, d), lambda i, j: (i, 0)),
                  pl.BlockSpec((1, d), lambda i, j: (0, 0)),
                  pl.BlockSpec((1, 1, d), lambda i, j: ((i // per_b) * N_MOD + 1, 0, 0)),
                  pl.BlockSpec((1, 1, d), lambda i, j: ((i // per_b) * N_MOD + 0, 0, 0)),
                  pl.BlockSpec((d, tn), lambda i, j: (0, j)),
                  pl.BlockSpec((d, V7X_LANES), lambda i, j: (0, 0))],
        out_specs=[pl.BlockSpec((tm, tn), lambda i, j: (i, j)),
                   pl.BlockSpec((tm, V7X_LANES), lambda i, j: (i, 0))],
        out_shape=[jax.ShapeDtypeStruct((t, n), BF16),
                   jax.ShapeDtypeStruct((t, V7X_LANES), F32)],
        scratch_shapes=[pltpu.VMEM((tm, d), BF16), pltpu.VMEM((8, d), F32)],
        compiler_params=_params("arbitrary", "arbitrary"),
        name="inproj",
    )(x2, g, mod3, mod3, w_all, w_gl)


def _mixer_a_kernel(u_ref, v_ref, lng_ref, lnb_ref, ws_ref, bs_ref, og_ref, o_ref, y_ref):
    tc, aw = u_ref.shape
    nh, c, _ = ws_ref.shape
    hd = aw // nh
    row = lax.broadcasted_iota(jnp.int32, (c, c), 0)
    col = lax.broadcasted_iota(jnp.int32, (c, c), 1)
    causal = col <= row
    for ci in range(tc // c):
        rows = slice(ci * c, (ci + 1) * c)
        v = v_ref[rows, :].astype(F32)
        mu = jnp.mean(v, axis=-1, keepdims=True)
        vc = v - mu
        var = jnp.mean(vc * vc, axis=-1, keepdims=True)
        vn = (vc * lax.rsqrt(var + EPS) * lng_ref[...] + lnb_ref[...]).astype(BF16)
        ssq = jnp.zeros((c, 1), F32)
        for h in range(nh):
            cols = slice(h * hd, (h + 1) * hd)
            w = jnp.where(causal, ws_ref[h], 0.0).astype(BF16)
            s = jnp.dot(w, vn[:, cols], preferred_element_type=F32) + bs_ref[:, h:h + 1]
            y = u_ref[rows, cols].astype(F32) * s
            y_ref[:, cols] = y
            ssq = ssq + jnp.sum(y * y, axis=-1, keepdims=True)
        inv = lax.rsqrt(ssq / aw + EPS)
        o_ref[rows, :] = (y_ref[...] * inv * og_ref[...]).astype(o_ref.dtype)


def _mixer_a(proj, ln_g, ln_b, w_s, b_s, out_g, aw, tc=256):
    t = proj.shape[0]
    nh, c, _ = w_s.shape
    return pl.pallas_call(
        _mixer_a_kernel,
        grid=(t // tc,),
        in_specs=[pl.BlockSpec((tc, aw), lambda i: (i, 0)),
                  pl.BlockSpec((tc, aw), lambda i: (i, 1)),
                  pl.BlockSpec((1, aw), lambda i: (0, 0)),
                  pl.BlockSpec((1, aw), lambda i: (0, 0)),
                  pl.BlockSpec((nh, c, c), lambda i: (0, 0, 0)),
                  pl.BlockSpec((c, nh), lambda i: (0, 0)),
                  pl.BlockSpec((1, aw), lambda i: (0, 0))],
        out_specs=pl.BlockSpec((tc, aw), lambda i: (i, 0)),
        out_shape=jax.ShapeDtypeStruct((t, aw), BF16),
        scratch_shapes=[pltpu.VMEM((c, aw), F32)],
        compiler_params=_params("arbitrary"),
        name="mixer_a",
    )(proj, proj, ln_g, ln_b, w_s, b_s.T, out_g)


def _split_bf16(x, terms):
    out = []
    for _ in range(terms - 1):
        hi = x.astype(BF16)
        out.append(hi)
        x = x - hi.astype(F32)
    out.append(x.astype(BF16))
    return out


_NT = (((1,), (1,)), ((), ()))
_TN = (((0,), (0,)), ((), ()))


def _mixer_b_kernel(q_ref, k_ref, v_ref, r_ref, gl_ref, wg2_ref, bg2_ref, hg_ref,
                    o_ref, st_ref, cum_ref):
    tb, kw = q_ref.shape
    vw = v_ref.shape[1]
    nh = st_ref.shape[0]
    dk, dv = kw // nh, vw // nh
    c = B_CHUNK
    dot = functools.partial(jnp.dot, preferred_element_type=F32)

    @pl.when(pl.program_id(1) == 0)
    def _():
        st_ref[...] = jnp.zeros_like(st_ref)

    gl_hi, gl_lo = _split_bf16(gl_ref[...], 2)
    w_hi, w_lo = _split_bf16(wg2_ref[...], 2)
    logits = dot(gl_hi, w_hi) + dot(gl_hi, w_lo) + dot(gl_lo, w_hi) + bg2_ref[...]
    log_a = jax.nn.log_sigmoid(logits) / B_GATE_TAU
    row = lax.broadcasted_iota(jnp.int32, (tb, tb), 0)
    col = lax.broadcasted_iota(jnp.int32, (tb, tb), 1)
    chunk_start = row - jnp.bitwise_and(row, c - 1)
    tri = jnp.where(col <= row, jnp.where(col >= chunk_start, 1.0, 0.0), 0.0).astype(BF16)
    cum = None
    for part in _split_bf16(log_a, 3):
        cum = dot(tri, part) if cum is None else cum + dot(tri, part)
    cum_ref[...] = cum

    crow = lax.broadcasted_iota(jnp.int32, (c, c), 0)
    ccol = lax.broadcasted_iota(jnp.int32, (c, c), 1)
    causal = ccol <= crow
    for ci in range(tb // c):
        rows = slice(ci * c, (ci + 1) * c)
        for h in range(nh):
            ks = slice(h * dk, (h + 1) * dk)
            vs = slice(h * dv, (h + 1) * dv)
            cum_h = cum_ref[rows, ks]
            last = cum_h[c - 1:c, :]
            q = q_ref[rows, ks].astype(F32) * (dk ** -0.5)
            k = k_ref[rows, ks].astype(F32)
            q_dec = (q * jnp.exp(cum_h)).astype(BF16)
            k_dec = (k * jnp.exp(-cum_h)).astype(BF16)
            k_state = (k * jnp.exp(last - cum_h)).astype(BF16)
            vh = v_ref[rows, vs]
            attn = lax.dot_general(q_dec, k_dec, _NT, preferred_element_type=F32)
            attn = jnp.where(causal, attn, 0.0).astype(BF16)
            state_t = st_ref[h]
            o = dot(attn, vh) + lax.dot_general(q_dec, state_t.astype(BF16), _NT,
                                                preferred_element_type=F32)
            kv_t = lax.dot_general(vh, k_state, _TN, preferred_element_type=F32)
            st_ref[h] = state_t * jnp.exp(last) + kv_t
            o = _rms_rows(o, hg_ref[...])
            o_ref[rows, vs] = (o * _silu(r_ref[rows, vs].astype(F32))).astype(o_ref.dtype)


def _mixer_b(proj, g_low, w_g2p, b_g2, head_g, bsz, seq, kw, vw, tb=256):
    t = proj.shape[0]
    per_b = seq // tb
    q_blk = (2 * vw) // kw
    v_blk = (2 * vw + 2 * kw) // vw
    row = lambda b, n: b * per_b + n
    return pl.pallas_call(
        _mixer_b_kernel,
        grid=(bsz, per_b),
        in_specs=[pl.BlockSpec((tb, kw), lambda b, n: (row(b, n), q_blk)),
                  pl.BlockSpec((tb, kw), lambda b, n: (row(b, n), q_blk + 1)),
                  pl.BlockSpec((tb, vw), lambda b, n: (row(b, n), v_blk)),
                  pl.BlockSpec((tb, vw), lambda b, n: (row(b, n), v_blk + 1)),
                  pl.BlockSpec((tb, V7X_LANES), lambda b, n: (row(b, n), 0)),
                  pl.BlockSpec((V7X_LANES, kw), lambda b, n: (0, 0)),
                  pl.BlockSpec((1, kw), lambda b, n: (0, 0)),
                  pl.BlockSpec((1, vw // B_HEADS), lambda b, n: (0, 0))],
        out_specs=pl.BlockSpec((tb, vw), lambda b, n: (row(b, n), 0)),
        out_shape=jax.ShapeDtypeStruct((t, vw), BF16),
        scratch_shapes=[pltpu.VMEM((B_HEADS, vw // B_HEADS, kw // B_HEADS), F32),
                        pltpu.VMEM((tb, kw), F32)],
        compiler_params=_params("arbitrary", "arbitrary"),
        name="mixer_b",
    )(proj, proj, proj, proj, g_low, w_g2p, b_g2, head_g)


def _fold_lanes(v):
    out = v[:, :V7X_LANES]
    for q in range(1, v.shape[1] // V7X_LANES):
        out = out + v[:, q * V7X_LANES:(q + 1) * V7X_LANES]
    return out


def _inv_rms(part, d):
    return lax.rsqrt(jnp.sum(part, axis=-1, keepdims=True) / d + EPS)


def _outmm_kernel(ya_ref, yb_ref, wa_ref, wb_ref, o_ref):
    o_ref[...] = (jnp.dot(ya_ref[...], wa_ref[...], preferred_element_type=F32)
                  + jnp.dot(yb_ref[...], wb_ref[...], preferred_element_type=F32))


def _outmm(ya, yb, w_out, tm=1024, tn=1024):
    t, aw = ya.shape
    d = w_out.shape[1]
    return pl.pallas_call(
        _outmm_kernel,
        grid=(t // tm, d // tn),
        in_specs=[pl.BlockSpec((tm, aw), lambda i, j: (i, 0)),
                  pl.BlockSpec((tm, aw), lambda i, j: (i, 0)),
                  pl.BlockSpec((aw, tn), lambda i, j: (0, j)),
                  pl.BlockSpec((aw, tn), lambda i, j: (1, j))],
        out_specs=pl.BlockSpec((tm, tn), lambda i, j: (i, j)),
        out_shape=jax.ShapeDtypeStruct((t, d), F32),
        compiler_params=_params("arbitrary", "arbitrary"),
        name="outmm",
    )(ya, yb, w_out, w_out)


def _postmix_kernel(y_ref, x_ref, gpost_ref, gate_ref, gpre_ref, sc_ref, sh_ref, wr_ref,
                    x1_ref, h2b_ref, h2p_ref, lg_ref, h2lo_ref, vec_ref):
    tm, d = x_ref.shape
    nr, pitch = _slab_rows(d), _slab_pitch(d)
    vec_ref[0:1, :] = gpost_ref[...] * gate_ref[0]
    vec_ref[1:2, :] = gpre_ref[...] * (1.0 + sc_ref[0])
    vec_ref[2:3, :] = sh_ref[0]

    def rows_of(r, u):
        return pl.multiple_of((r * NORM_UNROLL + u) * NORM_ROWS, NORM_ROWS)

    def body(r, carry):
        groups = [rows_of(r, u) for u in range(NORM_UNROLL)]
        invs = []
        for r0 in groups:
            rows = pl.ds(r0, NORM_ROWS)
            part = jnp.zeros((NORM_ROWS, V7X_LANES), F32)
            for c0 in range(0, d, NORM_COLS):
                y = y_ref[rows, c0:c0 + NORM_COLS]
                part = part + _fold_lanes(y * y)
            invs.append(_inv_rms(part, d))
        invs2 = []
        for r0, inv in zip(groups, invs):
            rows = pl.ds(r0, NORM_ROWS)
            part = jnp.zeros((NORM_ROWS, V7X_LANES), F32)
            for c0 in range(0, d, NORM_COLS):
                cols = slice(c0, c0 + NORM_COLS)
                x1 = x_ref[rows, cols] + (y_ref[rows, cols] * inv) * vec_ref[0:1, cols]
                x1_ref[rows, cols] = x1
                part = part + _fold_lanes(x1 * x1)
            invs2.append(_inv_rms(part, d))
        for r0, inv in zip(groups, invs2):
            rows = pl.ds(r0, NORM_ROWS)
            for c0 in range(0, d // 2, NORM_COLS):
                halves = []
                for cols in (slice(c0, c0 + NORM_COLS),
                             slice(d // 2 + c0, d // 2 + c0 + NORM_COLS)):
                    h2 = (x1_ref[rows, cols] * inv) * vec_ref[1:2, cols] + vec_ref[2:3, cols]
                    hb = h2.astype(BF16)
                    h2b_ref[rows, cols] = hb
                    hb32 = hb.astype(F32)
                    h2lo_ref[rows, cols] = (h2 - hb32).astype(BF16)
                    halves.append(hb32)
                words = _pack_pair(*halves)
                for q in range(NORM_COLS // V7X_LANES):
                    s = c0 // V7X_LANES + q
                    h2p_ref[pl.ds(r0 * pitch + s, NORM_ROWS, stride=pitch), :] = (
                        words[:, q * V7X_LANES:(q + 1) * V7X_LANES])
            for s in range(nr, pitch):
                h2p_ref[pl.ds(r0 * pitch + s, NORM_ROWS, stride=pitch), :] = (
                    jnp.zeros((NORM_ROWS, V7X_LANES), U32))
        return carry

    lax.fori_loop(0, tm // (NORM_ROWS * NORM_UNROLL), body, 0)
    w_hi, w_lo = _split_bf16(wr_ref[...], 2)
    p = jnp.dot(h2b_ref[...], jnp.concatenate([w_hi, w_lo], axis=1),
                preferred_element_type=F32)
    p_lo = jnp.dot(h2lo_ref[...], w_hi, preferred_element_type=F32)
    lg_ref[...] = p[:, :V7X_LANES] + p[:, V7X_LANES:] + p_lo


def _postmix(ymix, x2, gpost, mod3, gpre, w_rp, seq, tm=256):
    t, d = x2.shape
    per_b = seq // tm
    pitch = _slab_pitch(d)
    modspec = lambda m: pl.BlockSpec((1, 1, d), lambda i: ((i // per_b) * N_MOD + m, 0, 0))
    vecspec = pl.BlockSpec((1, d), lambda i: (0, 0))
    return pl.pallas_call(
        _postmix_kernel,
        grid=(t // tm,),
        in_specs=[pl.BlockSpec((tm, d), lambda i: (i, 0)),
                  pl.BlockSpec((tm, d), lambda i: (i, 0)),
                  vecspec, modspec(2), vecspec, modspec(4), modspec(3),
                  pl.BlockSpec((d, V7X_LANES), lambda i: (0, 0))],
        out_specs=[pl.BlockSpec((tm, d), lambda i: (i, 0)),
                   pl.BlockSpec((tm, d), lambda i: (i, 0)),
                   pl.BlockSpec((tm * pitch, V7X_LANES), lambda i: (i, 0)),
                   pl.BlockSpec((tm, V7X_LANES), lambda i: (i, 0))],
        out_shape=[jax.ShapeDtypeStruct((t, d), F32),
                   jax.ShapeDtypeStruct((t, d), BF16),
                   jax.ShapeDtypeStruct((t * pitch, V7X_LANES), U32),
                   jax.ShapeDtypeStruct((t, V7X_LANES), F32)],
        scratch_shapes=[pltpu.VMEM((tm, d), BF16),
                        pltpu.VMEM((8, d), F32)],
        compiler_params=_params("arbitrary"),
        name="postmix",
    )(ymix, x2, gpost, mod3, gpre, mod3, mod3, w_rp)


def _first_argmax(x, iota, axis, size):
    m = jnp.max(x, axis=axis, keepdims=True)
    idx = jnp.min(jnp.where(x == m, iota, size), axis=axis, keepdims=True)
    return m, idx


def _route_kernel(lg_ref, bias_ref, idx_ref, w_ref, rank_ref, cnt_ref, seen_ref):
    tt = lg_ref.shape[0]
    ne, ng = N_EXPERTS, N_GROUPS

    @pl.when(pl.program_id(0) == 0)
    def _():
        seen_ref[...] = jnp.zeros_like(seen_ref)

    gs = ne // ng
    neg = -jnp.inf
    scores = jax.nn.sigmoid(lg_ref[...].T[:ne, :])
    sel = scores + bias_ref[...]
    g3 = sel.reshape(ng, gs, tt)
    j_iota = lax.broadcasted_iota(jnp.int32, (ng, gs, tt), 1)
    m1, i1 = _first_argmax(g3, j_iota, 1, gs)
    m2 = jnp.max(jnp.where(j_iota == i1, neg, g3), axis=1, keepdims=True)
    grp = (m1 + m2).reshape(ng, tt)
    g_iota = lax.broadcasted_iota(jnp.int32, (ng, tt), 0)
    keep = jnp.zeros((ng, tt), jnp.bool_)
    for _ in range(TOPK_GROUPS):
        _, gi = _first_argmax(grp, g_iota, 0, ng)
        hit = g_iota == gi
        keep = jnp.logical_or(keep, hit)
        grp = jnp.where(hit, neg, grp)
    keep3 = jnp.broadcast_to(keep.reshape(ng, 1, tt), (ng, gs, tt))
    cand = jnp.where(keep3, g3, neg).reshape(ne, tt)
    e_iota = lax.broadcasted_iota(jnp.int32, (ne, tt), 0)
    idxs, ws, hits = [], [], []
    for _ in range(TOP_K):
        _, ei = _first_argmax(cand, e_iota, 0, ne)
        hit = e_iota == ei
        idxs.append(ei)
        hits.append(hit)
        ws.append(jnp.sum(jnp.where(hit, scores, 0.0), axis=0, keepdims=True))
        cand = jnp.where(hit, neg, cand)
    total = ws[0]
    for w in ws[1:]:
        total = total + w
    pad = idx_ref.shape[0] - TOP_K
    idx_ref[...] = jnp.concatenate(idxs + [jnp.zeros((pad, tt), jnp.int32)], axis=0)
    w_ref[...] = jnp.concatenate([w / total * ROUTED_SCALE for w in ws]
                                 + [jnp.zeros((pad, tt), F32)], axis=0)

    chosen = hits[0]
    for hit in hits[1:]:
        chosen = jnp.logical_or(chosen, hit)
    chosen = jnp.where(chosen, 1.0, 0.0)
    src = lax.broadcasted_iota(jnp.int32, (tt, tt), 0)
    dst = lax.broadcasted_iota(jnp.int32, (tt, tt), 1)
    before = jnp.where(src < dst, 1.0, 0.0).astype(BF16)
    rank_e = seen_ref[:, 0:1] + jnp.dot(chosen.astype(BF16), before,
                                        preferred_element_type=F32)
    ranks = [jnp.sum(jnp.where(hit, rank_e, 0.0), axis=0, keepdims=True) for hit in hits]
    rank_ref[...] = jnp.concatenate(ranks + [jnp.zeros((pad, tt), F32)],
                                    axis=0).astype(jnp.int32)
    seen_ref[...] = seen_ref[...] + jnp.sum(chosen, axis=1, keepdims=True)
    cnt_ref[...] = seen_ref[...].astype(jnp.int32)


def _route(logits, bias, tt=512):
    t = logits.shape[0]
    tok_spec = pl.BlockSpec((8, tt), lambda i: (0, i))
    return pl.pallas_call(
        _route_kernel,
        grid=(t // tt,),
        in_specs=[pl.BlockSpec((tt, V7X_LANES), lambda i: (i, 0)),
                  pl.BlockSpec((N_EXPERTS, 1), lambda i: (0, 0))],
        out_specs=[tok_spec, tok_spec, tok_spec,
                   pl.BlockSpec((N_EXPERTS, V7X_LANES), lambda i: (0, 0))],
        out_shape=[jax.ShapeDtypeStruct((8, t), jnp.int32),
                   jax.ShapeDtypeStruct((8, t), F32),
                   jax.ShapeDtypeStruct((8, t), jnp.int32),
                   jax.ShapeDtypeStruct((N_EXPERTS, V7X_LANES), jnp.int32)],
        scratch_shapes=[pltpu.VMEM((N_EXPERTS, V7X_LANES), F32)],
        compiler_params=_params("arbitrary"),
        name="route",
    )(logits, bias.reshape(N_EXPERTS, 1))


def _plan_kernel(idx_ref, rank_ref, cnt_ref, dest_ref, be_ref, nr_ref, tgt_ref, ws_ref,
                 *, bm, nch):
    ne = N_EXPERTS
    shift = bm.bit_length() - 1
    counts = cnt_ref[...]
    padded = ((counts + (bm - 1)) >> shift) << shift
    r = lax.broadcasted_iota(jnp.int32, (ne, ne), 0)
    c = lax.broadcasted_iota(jnp.int32, (ne, ne), 1)
    upto = jnp.where(c <= r, 1.0, 0.0)
    pend = jnp.dot(upto, padded.astype(F32), precision=HIGHEST,
                   preferred_element_type=F32).astype(jnp.int32)
    pstart = pend - padded
    idx = idx_ref[...]
    dest = rank_ref[...]
    blk = lax.broadcasted_iota(jnp.int32, be_ref.shape, 1)
    blk_row0 = blk * bm
    blk_e = jnp.zeros(be_ref.shape, jnp.int32)
    lane_e = lax.broadcasted_iota(jnp.int32, ws_ref.shape, 1)
    used_before = jnp.zeros(ws_ref.shape, jnp.int32)
    for e in range(ne):
        dest = dest + jnp.where(idx == e, pstart[e:e + 1, 0:1], 0)
        blk_e = blk_e + jnp.where(pend[e:e + 1, 0:1] <= blk_row0, 1, 0)
        used_before = used_before + jnp.where(
            lane_e > e, jnp.where(counts[e:e + 1, 0:1] > 0, 1, 0), 0)
    dest_ref[...] = dest
    blk_e = jnp.minimum(blk_e, ne - 1)
    be_ref[...] = blk_e
    nr_ref[...] = jnp.broadcast_to(pend[ne - 1:ne, :] >> shift, nr_ref.shape)
    ws_ref[...] = jnp.bitwise_and(used_before, 1)

    seg_end = jnp.zeros(be_ref.shape, jnp.int32)
    seg_start = jnp.zeros(be_ref.shape, jnp.int32)
    for e in range(ne):
        mine = blk_e == e
        seg_end = seg_end + jnp.where(mine, pend[e:e + 1, 0:1], 0)
        seg_start = seg_start + jnp.where(mine, pstart[e:e + 1, 0:1], 0)
    nxt = jnp.zeros(be_ref.shape, jnp.int32)
    for e in range(ne):
        nxt = nxt + jnp.where(pend[e:e + 1, 0:1] <= seg_end, 1, 0)
    need = jnp.where(nxt < ne, (nxt - blk_e) * nch, 0)
    n_blk = jnp.maximum((seg_end - seg_start) >> shift, 1)
    j = blk - (seg_start >> shift)
    per = (need.astype(F32) / n_blk.astype(F32)).astype(jnp.int32)
    per = per + jnp.where((per + 1) * n_blk <= need, 1, 0)
    per = per - jnp.where(per * n_blk > need, 1, 0)
    per = per + jnp.where(per * n_blk < need, 1, 0)
    tgt_ref[...] = (blk_e + 1) * nch + jnp.minimum(need, (j + 1) * per)


def _plan(idx8, rank8, cnt, bm, nb, nch, tt=2048):
    t = idx8.shape[1]
    tt = min(tt, t)
    nbp = -(-nb // V7X_LANES) * V7X_LANES
    tok_spec = pl.BlockSpec((8, tt), lambda i: (0, i))
    blk_spec = pl.BlockSpec((8, nbp), lambda i: (0, 0))
    one_spec = pl.BlockSpec((8, V7X_LANES), lambda i: (0, 0))
    return pl.pallas_call(
        functools.partial(_plan_kernel, bm=bm, nch=nch),
        grid=(t // tt,),
        in_specs=[tok_spec, tok_spec,
                  pl.BlockSpec((N_EXPERTS, V7X_LANES), lambda i: (0, 0))],
        out_specs=[tok_spec, blk_spec, one_spec, blk_spec, one_spec],
        out_shape=[jax.ShapeDtypeStruct((8, t), jnp.int32),
                   jax.ShapeDtypeStruct((8, nbp), jnp.int32),
                   jax.ShapeDtypeStruct((8, V7X_LANES), jnp.int32),
                   jax.ShapeDtypeStruct((8, nbp), jnp.int32),
                   jax.ShapeDtypeStruct((8, V7X_LANES), jnp.int32)],
        compiler_params=_params("arbitrary"),
        name="plan",
    )(idx8, rank8, cnt)


def _row_gather(src_hbm, dst, sem, src_tok, dst_tok, nr, pitch):
    return pltpu.make_async_copy(src_hbm.at[pl.ds(src_tok * pitch, nr)],
                                 dst.at[pl.ds(dst_tok * pitch, nr)], sem)


def _gather_wait(src_hbm, dst, sem, n_tok, nr):
    pltpu.make_async_copy(src_hbm.at[pl.ds(0, n_tok * nr)],
                          dst.at[pl.ds(0, n_tok * nr)], sem).wait()


WCHUNK_ROWS = 512
WEIGHT_DMA_PRIORITY = 1


def _wchunks(d):
    each = d // WCHUNK_ROWS
    return each, 3 * each


def _experts_kernel(be_ref, nr_ref, st_ref, tgt_ref, ws_ref,
                    h2s_hbm, wg_hbm, wu_hbm, wd_hbm, y_ref,
                    xbuf, sem, x_ref, wg_b, wu_b, wd_b, stg_a, stg_b, wsem, done_ref):
    i = pl.program_id(0)
    n_real = nr_ref[0]
    _, d, f = wg_b.shape
    nr, pitch = _slab_rows(d), _slab_pitch(d)
    bm = xbuf.shape[1] // pitch
    each, nch = _wchunks(d)
    dn_rows = f // each
    slot = lax.rem(i, 2)
    e_cur = be_ref[jnp.maximum(jnp.minimum(i, n_real - 1), 0)]
    limit = (be_ref[jnp.maximum(n_real - 1, 0)] + 1) * nch

    def start_gather(blk, sl):
        def body(r, carry):
            tok = st_ref[blk * bm + r]
            _row_gather(h2s_hbm, xbuf.at[sl], sem.at[sl], tok, r, nr, pitch).start()
            return carry
        lax.fori_loop(0, bm, body, 0, unroll=8)

    def chunk_parts(g):
        ex = g // nch
        c = g - ex * nch
        return ex, c

    def chunk_rows(kind, c):
        n = dn_rows if kind == 2 else WCHUNK_ROWS
        return pl.ds(lax.rem(c, each) * n, n)

    def chunk_copy(kind, ex, c, sl):
        src = (wg_hbm, wu_hbm, wd_hbm)[kind]
        stg = stg_b if kind == 2 else stg_a
        return pltpu.make_async_copy(src.at[ex, chunk_rows(kind, c), :], stg.at[sl],
                                     wsem.at[sl])

    def chunk_start(g, sl):
        ex, c = chunk_parts(g)
        for kind in range(3):
            @pl.when(c // each == kind)
            def _():
                chunk_copy(kind, ex, c, sl).start(priority=WEIGHT_DMA_PRIORITY)

    def chunk_finish(g, sl):
        ex, c = chunk_parts(g)
        wsl = ws_ref[ex]
        for kind in range(3):
            @pl.when(c // each == kind)
            def _():
                chunk_copy(kind, ex, c, sl).wait()
                stg = stg_b if kind == 2 else stg_a
                dst = (wg_b, wu_b, wd_b)[kind]
                dst[wsl, chunk_rows(kind, c), :] = stg[sl].astype(BF16)

    def convert_until(target):
        def body(g, carry):
            sl = lax.rem(g, 2)
            chunk_finish(g, sl)

            @pl.when(g + 2 < limit)
            def _():
                chunk_start(g + 2, sl)
            return carry
        lax.fori_loop(done_ref[0], target, body, 0)
        done_ref[0] = jnp.maximum(done_ref[0], target)

    @pl.when(i == 0)
    def _():
        done_ref[0] = 0
        chunk_start(0, 0)
        chunk_start(1, 1)
        start_gather(0, 0)

    @pl.when(i + 1 < n_real)
    def _():
        start_gather(i + 1, 1 - slot)

    @pl.when(i < n_real)
    def _():
        convert_until((e_cur + 1) * nch)
        wsl = ws_ref[e_cur]
        _gather_wait(h2s_hbm, xbuf.at[slot], sem.at[slot], bm, nr)
        for kc in range(nr // SLAB_GROUP):
            lo, hi = _unpack_pair(_slab_cols(xbuf.at[slot], 0, bm, pitch, kc))
            x_ref[:, kc * SLAB_K:(kc + 1) * SLAB_K] = lo.astype(BF16)
            x_ref[:, d // 2 + kc * SLAB_K:d // 2 + (kc + 1) * SLAB_K] = hi.astype(BF16)
        x = x_ref[...]
        g = jnp.dot(x, wg_b[wsl], preferred_element_type=F32)
        u = jnp.dot(x, wu_b[wsl], preferred_element_type=F32)
        a = (_silu(g) * u).astype(BF16)
        for kc in range(nr // SLAB_GROUP):
            halves = [_round_bf16(jnp.dot(a, wd_b[wsl, :, c0:c0 + SLAB_K],
                                          preferred_element_type=F32))
                      for c0 in (kc * SLAB_K, d // 2 + kc * SLAB_K)]
            words = _pack_pair(*halves)
            for q in range(SLAB_GROUP):
                y_ref[pl.ds(kc * SLAB_GROUP + q, bm, stride=pitch), :] = (
                    words[:, q * V7X_LANES:(q + 1) * V7X_LANES])
        for s in range(nr, pitch):
            y_ref[pl.ds(s, bm, stride=pitch), :] = jnp.zeros((bm, V7X_LANES), U32)
        convert_until(tgt_ref[i])

    @pl.when(i >= n_real)
    def _():
        y_ref[...] = jnp.zeros_like(y_ref)


def _experts(block_e, n_real, slot_tok, tgt, wslot, h2s, wg, wu, wd, bm):
    nb = block_e.shape[0]
    ne, d, f = wg.shape
    pitch = _slab_pitch(d)
    each, _ = _wchunks(d)
    hbm = pl.BlockSpec(memory_space=pl.ANY)
    return pl.pallas_call(
        _experts_kernel,
        grid_spec=pltpu.PrefetchScalarGridSpec(
            num_scalar_prefetch=5,
            grid=(nb,),
            in_specs=[hbm, hbm, hbm, hbm],
            out_specs=pl.BlockSpec((bm * pitch, V7X_LANES), lambda i, *_: (i, 0)),
            scratch_shapes=[pltpu.VMEM((2, bm * pitch, V7X_LANES), U32),
                            pltpu.SemaphoreType.DMA((2,)),
                            pltpu.VMEM((bm, d), BF16),
                            pltpu.VMEM((2, d, f), BF16),
                            pltpu.VMEM((2, d, f), BF16),
                            pltpu.VMEM((2, f, d), BF16),
                            pltpu.VMEM((2, WCHUNK_ROWS, f), F32),
                            pltpu.VMEM((2, f // each, d), F32),
                            pltpu.SemaphoreType.DMA((2,)),
                            pltpu.SMEM((1,), jnp.int32)]),
        out_shape=jax.ShapeDtypeStruct((nb * bm * pitch, V7X_LANES), U32),
        compiler_params=_params("arbitrary"),
        name="experts",
    )(block_e, n_real, slot_tok, tgt, wslot, h2s, wg, wu, wd)


def _shared_kernel(cnt_ref, dest_ref, h_ref, wg_ref, wu_ref, wd_ref, o_ref, st_ref, *, bm):
    i = pl.program_id(0)
    tm = h_ref.shape[0]
    ns = st_ref.shape[0]

    @pl.when(i == 0)
    def _():
        def clear(lo, hi):
            def body(s, carry):
                st_ref[s] = 0
                return carry
            lax.fori_loop(lo, hi, body, 0)

        def per_expert(e, seg_start):
            n = cnt_ref[e]
            seg_end = seg_start + (n + (bm - 1)) // bm * bm
            clear(seg_start + n, seg_end)
            return seg_end

        used = lax.fori_loop(0, N_EXPERTS, per_expert, 0)
        clear(used, ns)

    for t in range(tm):
        for k in range(TOP_K):
            st_ref[dest_ref[k * tm + t]] = i * tm + t

    h = h_ref[...]
    g = jnp.dot(h, wg_ref[...], preferred_element_type=F32)
    u = jnp.dot(h, wu_ref[...], preferred_element_type=F32)
    a = (_silu(g) * u).astype(BF16)
    o_ref[...] = jnp.dot(a, wd_ref[...], preferred_element_type=F32)


def _shared(counts, dest8, h2b, wg, wu, wd, bm, nb, tm=256):
    t, d = h2b.shape
    f = wg.shape[1]
    rows = dest8.shape[0]
    dest_tiles = dest8.reshape(rows, t // tm, tm).transpose(1, 0, 2).reshape(-1)
    return pl.pallas_call(
        functools.partial(_shared_kernel, bm=bm),
        grid=(t // tm,),
        in_specs=[pl.BlockSpec(memory_space=pltpu.SMEM),
                  pl.BlockSpec((rows * tm,), lambda i: (i,), memory_space=pltpu.SMEM),
                  pl.BlockSpec((tm, d), lambda i: (i, 0)),
                  pl.BlockSpec((d, f), lambda i: (0, 0)),
                  pl.BlockSpec((d, f), lambda i: (0, 0)),
                  pl.BlockSpec((f, d), lambda i: (0, 0))],
        out_specs=[pl.BlockSpec((tm, d), lambda i: (i, 0)),
                   pl.BlockSpec(memory_space=pltpu.SMEM)],
        out_shape=[jax.ShapeDtypeStruct((t, d), F32),
                   jax.ShapeDtypeStruct((nb * bm,), jnp.int32)],
        compiler_params=_params("arbitrary"),
        name="shared",
    )(counts, dest_tiles, h2b, wg, wu, wd)


def _combine_kernel(dest_ref, ys_hbm, w_ref, sh_ref, x1_ref, g_ref, gate_ref,
                    o_ref, gbuf0, gbuf1, ybuf, vec_ref, sem):
    gbufs = (gbuf0, gbuf1)
    i = pl.program_id(0)
    n = pl.num_programs(0)
    tt, d = x1_ref.shape
    nr, pitch = _slab_rows(d), _slab_pitch(d)
    slot = lax.rem(i, 2)

    def gather_row(blk, buf, r, k, priority=0):
        row = dest_ref[k * (n * tt) + blk * tt + r]
        _row_gather(ys_hbm, gbufs[buf], sem.at[buf], row, k * tt + r,
                    nr, pitch).start(priority=priority)

    @pl.when(i == 0)
    def _():
        def body(r, carry):
            for k in range(TOP_K):
                gather_row(0, 0, r, k)
            return carry
        lax.fori_loop(0, tt, body, 0, unroll=4)

    def weighted_sum(cur):
        nxt = 1 - cur
        _gather_wait(ys_hbm, gbufs[cur], sem.at[cur], TOP_K * tt, nr)
        nxt_blk = jnp.minimum(i + 1, n - 1)
        for r in range(tt):
            for k in range(TOP_K):
                gather_row(nxt_blk, nxt, r, k, priority=(r * TOP_K + k) % 2)
        wk = [jnp.broadcast_to(w_ref[:, k:k + 1], (tt, V7X_LANES)) for k in range(TOP_K)]
        for s in range(nr):
            cols_lo = slice(s * V7X_LANES, (s + 1) * V7X_LANES)
            cols_hi = slice(d // 2 + s * V7X_LANES, d // 2 + (s + 1) * V7X_LANES)
            acc_lo = sh_ref[:, cols_lo]
            acc_hi = sh_ref[:, cols_hi]
            for k in range(TOP_K):
                lo, hi = _unpack_pair(
                    gbufs[cur][pl.ds(k * tt * pitch + s, tt, stride=pitch), :])
                acc_lo = acc_lo + lo * wk[k]
                acc_hi = acc_hi + hi * wk[k]
            ybuf[:, cols_lo] = acc_lo
            ybuf[:, cols_hi] = acc_hi

        @pl.when(i + 1 == n)
        def _():
            _gather_wait(ys_hbm, gbufs[nxt], sem.at[nxt], TOP_K * tt, nr)

    for cur in range(2):
        @pl.when(slot == cur)
        def _():
            weighted_sum(cur)

    vec_ref[0:1, :] = g_ref[...] * gate_ref[0]

    def body(r, carry):
        groups = [pl.ds(pl.multiple_of((r * NORM_UNROLL + u) * NORM_ROWS, NORM_ROWS),
                        NORM_ROWS) for u in range(NORM_UNROLL)]
        invs = []
        for rows in groups:
            part = jnp.zeros((NORM_ROWS, V7X_LANES), F32)
            for c0 in range(0, d, NORM_COLS):
                yc = ybuf[rows, c0:c0 + NORM_COLS]
                part = part + _fold_lanes(yc * yc)
            invs.append(_inv_rms(part, d))
        for rows, inv in zip(groups, invs):
            for c0 in range(0, d, NORM_COLS):
                cols = slice(c0, c0 + NORM_COLS)
                o_ref[rows, cols] = (x1_ref[rows, cols]
                                     + (ybuf[rows, cols] * inv) * vec_ref[0:1, cols])
        return carry

    lax.fori_loop(0, tt // (NORM_ROWS * NORM_UNROLL), body, 0)


def _combine(dest, y_rows, wts, shared, x1, g, mod3, seq, tt=128):
    t, d = x1.shape
    pitch = _slab_pitch(d)
    per_b = seq // tt
    return pl.pallas_call(
        _combine_kernel,
        grid_spec=pltpu.PrefetchScalarGridSpec(
            num_scalar_prefetch=1,
            grid=(t // tt,),
            in_specs=[pl.BlockSpec(memory_space=pl.ANY),
                      pl.BlockSpec((tt, 8), lambda i, ds: (i, 0)),
                      pl.BlockSpec((tt, d), lambda i, ds: (i, 0)),
                      pl.BlockSpec((tt, d), lambda i, ds: (i, 0)),
                      pl.BlockSpec((1, d), lambda i, ds: (0, 0)),
                      pl.BlockSpec((1, 1, d), lambda i, ds: ((i // per_b) * N_MOD + 5, 0, 0))],
            out_specs=pl.BlockSpec((tt, d), lambda i, ds: (i, 0)),
            scratch_shapes=[pltpu.VMEM((TOP_K * tt * pitch, V7X_LANES), U32),
                            pltpu.VMEM((TOP_K * tt * pitch, V7X_LANES), U32),
                            pltpu.VMEM((tt, d), F32),
                            pltpu.VMEM((8, d), F32),
                            pltpu.SemaphoreType.DMA((2,))]),
        out_shape=jax.ShapeDtypeStruct((t, d), F32),
        compiler_params=_params("arbitrary"),
        name="combine",
    )(dest, y_rows, wts, shared, x1, g, mod3)


EXPERT_BLOCK_ROWS = 128


def kernel(x, c, w_ada, b_ada, g_pre_mix, g_post_mix, g_pre_ffn, g_post_ffn, w_in,
           a_ln_g, a_ln_b, a_w_s, a_b_s, a_out_g, b_w_g2, b_b_g2, b_head_g, w_out,
           w_router, router_bias, we_gate, we_up, we_down, ws_gate, ws_up, ws_down):
    bsz, seq, d = x.shape
    t = bsz * seq
    aw = d // 2
    vw = d - aw
    kw = vw // 2
    n_main = 2 * aw + 2 * kw + 2 * vw
    x2 = x.reshape(t, d)
    for l in range(w_ada.shape[0]):
        mod = _ada(c, w_ada[l], b_ada[l])
        mod3 = mod.reshape(bsz * N_MOD, 1, d)

        w_gl = jnp.pad(w_in[l][:, n_main:], ((0, 0), (0, V7X_LANES - B_GATE_RANK))).astype(BF16)
        proj, g_low = _inproj(x2, g_pre_mix[l][None], mod3, w_in[l].astype(BF16), n_main,
                              w_gl, seq)
        y_a = _mixer_a(proj, a_ln_g[l][None], a_ln_b[l][None], a_w_s[l], a_b_s[l],
                       a_out_g[l][None], aw)
        w_g2p = jnp.pad(b_w_g2[l], ((0, V7X_LANES - B_GATE_RANK), (0, 0)))
        y_b = _mixer_b(proj, g_low, w_g2p, b_b_g2[l][None], b_head_g[l][None],
                       bsz, seq, kw, vw)

        w_rp = jnp.pad(w_router[l], ((0, 0), (0, V7X_LANES - N_EXPERTS)))
        ymix = _outmm(y_a, y_b, w_out[l].astype(BF16))
        x1, h2b, h2s, logits = _postmix(ymix, x2, g_post_mix[l][None], mod3,
                                        g_pre_ffn[l][None], w_rp, seq)
        idx8, w8, rank8, cnt = _route(logits, router_bias[l])
        bm = EXPERT_BLOCK_ROWS
        nb = -(-t * TOP_K // bm) + N_EXPERTS
        dest8, be8, nr8, tgt8, ws8 = _plan(idx8, rank8, cnt, bm, nb, _wchunks(d)[1])
        shared, slot_tok = _shared(cnt[:, 0], dest8, h2b, ws_gate[l].astype(BF16),
                                   ws_up[l].astype(BF16), ws_down[l].astype(BF16), bm, nb)
        y_rows = _experts(be8[0, :nb], nr8[0, :1], slot_tok, tgt8[0, :nb], ws8[0, :N_EXPERTS],
                          h2s, we_gate[l], we_up[l], we_down[l], bm)
        x2 = _combine(dest8[:TOP_K].reshape(-1), y_rows, w8.T, shared, x1, g_post_ffn[l][None],
                      mod3, seq)
    return x2.reshape(bsz, seq, d)
```
